```python
import math
import jax, jax.numpy as jnp
from jax import lax
import numpy as np

D_MODEL = 2048
BATCH = 4
SEQ = 2048
DEPTH = 4

N_META = 16
D_MIX = D_MODEL
DA_HEADS = 8
DA_HEAD_DIM = 64
DA_WIDTH = DA_HEADS * 2 * DA_HEAD_DIM
GLA_HEADS = 4
GLA_WIDTH = D_MIX - DA_WIDTH
GLA_HEAD_V = GLA_WIDTH // GLA_HEADS
GLA_KEY_WIDTH = GLA_WIDTH // 2
GLA_HEAD_K = GLA_KEY_WIDTH // GLA_HEADS
GLA_GATE_RANK = 16
GLA_GATE_NORM = 16.0
GLA_CHUNK = 64
Q_BLOCK = 128
D_FF = 5632
CONV_W = 3
EPS = 1e-6
IN_SPLITS = (DA_WIDTH, DA_WIDTH, DA_WIDTH, GLA_KEY_WIDTH, GLA_KEY_WIDTH, GLA_WIDTH, GLA_WIDTH, GLA_GATE_RANK)
IN_COLS = sum(IN_SPLITS)

kernel_name = "hymba_diffattn_gla_convffn_trunk"


def rmsnorm(x, g):
    xf = x.astype(jnp.float32)
    y = xf * lax.rsqrt(jnp.mean(xf * xf, axis=-1, keepdims=True) + EPS) * g.astype(jnp.float32)
    return y.astype(x.dtype)


def split_cols(p):
    idx = np.cumsum(IN_SPLITS)[:-1].tolist()
    return jnp.split(p, idx, axis=-1)


def diff_attention(q, k, v, lam, subln_g, lam_init):
    dtype = q.dtype
    B, L, _ = q.shape
    Lp = -(-L // Q_BLOCK) * Q_BLOCK
    nb = Lp // Q_BLOCK
    pad = ((0, 0), (0, Lp - L), (0, 0))
    q = jnp.pad(q.astype(jnp.float32), pad).reshape(B, Lp, DA_HEADS, 2, DA_HEAD_DIM).transpose(0, 2, 3, 1, 4)
    k = jnp.pad(k.astype(jnp.float32), pad).reshape(B, Lp, DA_HEADS, 2, DA_HEAD_DIM).transpose(0, 2, 3, 1, 4)
    v = jnp.pad(v.astype(jnp.float32), pad).reshape(B, Lp, DA_HEADS, 2 * DA_HEAD_DIM).transpose(0, 2, 1, 3)
    q_blocks = q.reshape(B, DA_HEADS, 2, nb, Q_BLOCK, DA_HEAD_DIM).transpose(3, 0, 1, 2, 4, 5)
    kpos = jnp.arange(Lp)
    scale = DA_HEAD_DIM ** -0.5

    def block(args):
        qb, blk = args
        s = jnp.einsum('bhcqd,bhckd->bhcqk', qb, k) * scale
        qpos = blk * Q_BLOCK + jnp.arange(Q_BLOCK)
        mask = kpos[None, :] <= qpos[:, None]
        s = jnp.where(mask, s, -jnp.inf)
        p = jax.nn.softmax(s, axis=-1)
        p_diff = p[:, :, 0] - lam * p[:, :, 1]
        return jnp.einsum('bhqk,bhke->bhqe', p_diff, v)

    o = lax.map(block, (q_blocks, jnp.arange(nb)))
    o = o.transpose(1, 0, 3, 2, 4).reshape(B, Lp, DA_HEADS, 2 * DA_HEAD_DIM)[:, :L]
    o = rmsnorm(o, subln_g) * (1.0 - lam_init)
    return o.reshape(B, L, DA_WIDTH).astype(dtype)


def gla(q, k, v, g_out, gate_lr, w2, b2, norm_g):
    dtype = v.dtype
    B, L, _ = v.shape
    f32 = jnp.float32
    gk = jax.nn.log_sigmoid(gate_lr.astype(f32) @ w2.astype(f32) + b2.astype(f32)) / GLA_GATE_NORM
    P = GLA_CHUNK - N_META
    T = P + L
    N = T // GLA_CHUNK
    pad = ((0, 0), (P, 0), (0, 0))

    def chunks(t, hd):
        t = jnp.pad(t.astype(f32), pad).reshape(B, N, GLA_CHUNK, GLA_HEADS, hd)
        return t.transpose(0, 3, 1, 2, 4)

    qc = chunks(q, GLA_HEAD_K) * (GLA_HEAD_K ** -0.5)
    kc = chunks(k, GLA_HEAD_K)
    vc = chunks(v, GLA_HEAD_V)
    gc = chunks(gk, GLA_HEAD_K)
    b = jnp.cumsum(gc, axis=3)
    q_in = qc * jnp.exp(b)
    k_in = kc * jnp.exp(-b)
    A = jnp.einsum('bhnid,bhnjd->bhnij', q_in, k_in)
    tril = jnp.tril(jnp.ones((GLA_CHUNK, GLA_CHUNK), dtype=bool))
    A = jnp.where(tril, A, 0.0)
    o_intra = jnp.einsum('bhnij,bhnjv->bhniv', A, vc)
    b_last = b[:, :, :, -1:, :]
    chunk_kv = jnp.einsum('bhncd,bhncv->bhndv', kc * jnp.exp(b_last - b), vc)
    decay = jnp.exp(b_last[:, :, :, 0, :])

    def step(S, inp):
        dec, kv = inp
        return dec[..., None] * S + kv, S

    S0 = jnp.zeros((B, GLA_HEADS, GLA_HEAD_K, GLA_HEAD_V), f32)
    _, S_prev = lax.scan(step, S0, (jnp.moveaxis(decay, 2, 0), jnp.moveaxis(chunk_kv, 2, 0)))
    S_prev = jnp.moveaxis(S_prev, 0, 2)
    o_inter = jnp.einsum('bhncd,bhndv->bhncv', q_in, S_prev)
    o = (o_intra + o_inter).transpose(0, 2, 3, 1, 4).reshape(B, T, GLA_HEADS, GLA_HEAD_V)[:, P:]
    o = rmsnorm(o, norm_g).reshape(B, L, GLA_WIDTH)
    return (o * jax.nn.silu(g_out.astype(f32))).astype(dtype)


def conv_ffn(h, w_up, conv_w, conv_b, w_down):
    L = h.shape[1]
    u = h @ w_up
    up = jnp.pad(u, ((0, 0), (CONV_W - 1, 0), (0, 0)))
    uc = conv_b + sum(conv_w[i] * up[:, i:i + L] for i in range(CONV_W))
    a, val = jnp.split(uc, 2, axis=-1)
    return (jax.nn.silu(a) * val) @ w_down


def setup_inputs(seed: int = 0) -> dict:
    key = jax.random.key(seed)
    ks = jax.random.split(key, 20)
    f32 = jnp.float32

    def nrm(k, shape, scale):
        return jax.random.normal(k, shape, f32) * scale

    def gain(k, shape):
        return 1.0 + 0.05 * jax.random.normal(k, shape, f32)

    return {
        "x": nrm(ks[0], (BATCH, SEQ, D_MODEL), 1.0),
        "meta_tokens": nrm(ks[1], (N_META, D_MODEL), 1.0),
        "pre_mix_g": gain(ks[2], (DEPTH, D_MODEL)),
        "w_in": nrm(ks[3], (DEPTH, D_MODEL, IN_COLS), D_MODEL ** -0.5),
        "da_lambda": nrm(ks[4], (DEPTH, 4, DA_HEAD_DIM), 0.1),
        "da_subln_g": gain(ks[5], (DEPTH, 2 * DA_HEAD_DIM)),
        "gla_gate_w2": nrm(ks[6], (DEPTH, GLA_GATE_RANK, GLA_KEY_WIDTH), GLA_GATE_RANK ** -0.5),
        "gla_gate_b": nrm(ks[7], (DEPTH, GLA_KEY_WIDTH), 0.01),
        "gla_norm_g": gain(ks[8], (DEPTH, GLA_HEAD_V)),
        "w_out": nrm(ks[9], (DEPTH, D_MIX, D_MODEL), D_MIX ** -0.5),
        "post_mix_g": gain(ks[10], (DEPTH, D_MODEL)),
        "pre_ffn_g": gain(ks[11], (DEPTH, D_MODEL)),
        "w_up": nrm(ks[12], (DEPTH, D_MODEL, 2 * D_FF), D_MODEL ** -0.5),
        "conv_w": nrm(ks[13], (DEPTH, CONV_W, 2 * D_FF), CONV_W ** -0.5),
        "conv_b": nrm(ks[14], (DEPTH, 2 * D_FF), 0.01),
        "w_down": nrm(ks[15], (DEPTH, D_FF, D_MODEL), D_FF ** -0.5),
        "post_ffn_g": gain(ks[16], (DEPTH, D_MODEL)),
    }


def reference(x, meta_tokens, pre_mix_g, w_in, da_lambda, da_subln_g, gla_gate_w2, gla_gate_b,
              gla_norm_g, w_out, post_mix_g, pre_ffn_g, w_up, conv_w, conv_b, w_down, post_ffn_g):
    B = x.shape[0]
    meta = jnp.broadcast_to(meta_tokens.astype(x.dtype)[None], (B, N_META, D_MODEL))
    h_res = jnp.concatenate([meta, x], axis=1)
    for l in range(DEPTH):
        lam_init = 0.8 - 0.6 * math.exp(-0.3 * l)
        h = rmsnorm(h_res, pre_mix_g[l])
        q_da, k_da, v_da, q_g, k_g, v_g, g_g, lr_g = split_cols(h @ w_in[l])
        lamv = da_lambda[l].astype(jnp.float32)
        lam = jnp.exp(jnp.sum(lamv[0] * lamv[1])) - jnp.exp(jnp.sum(lamv[2] * lamv[3])) + lam_init
        o_da = diff_attention(q_da, k_da, v_da, lam, da_subln_g[l], lam_init)
        o_gla = gla(q_g, k_g, v_g, g_g, lr_g, gla_gate_w2[l], gla_gate_b[l], gla_norm_g[l])
        y = jnp.concatenate([o_da, o_gla], axis=-1) @ w_out[l]
        h_res = h_res + rmsnorm(y, post_mix_g[l])
        h = rmsnorm(h_res, pre_ffn_g[l])
        y = conv_ffn(h, w_up[l], conv_w[l], conv_b[l], w_down[l])
        h_res = h_res + rmsnorm(y, post_ffn_g[l])
    return h_res[:, N_META:]
```

```python
import functools
import math

import jax
import jax.numpy as jnp
from jax import lax
from jax.experimental import pallas as pl
from jax.experimental.pallas import tpu as pltpu

D_MODEL = 2048
BATCH = 4
SEQ = 2048
DEPTH = 4
N_META = 16

DA_HEADS = 8
DA_HEAD_DIM = 64
DA_HEAD_WIDTH = 2 * DA_HEAD_DIM
DA_WIDTH = DA_HEADS * DA_HEAD_WIDTH
GLA_HEADS = 4
GLA_WIDTH = 1024
GLA_HEAD_V = 256
GLA_KEY_WIDTH = 512
GLA_HEAD_K = 128
GLA_GATE_RANK = 16
GLA_GATE_NORM = 16.0
GLA_CHUNK = 64
D_FF = 5632
EPS = 1e-6

COL_Q_DA = 0
COL_K_DA = DA_WIDTH
COL_V_DA = 2 * DA_WIDTH
COL_Q_G = 3 * DA_WIDTH
COL_K_G = COL_Q_G + GLA_KEY_WIDTH
COL_V_G = COL_K_G + GLA_KEY_WIDTH
COL_G_G = COL_V_G + GLA_WIDTH
COL_LR = COL_G_G + GLA_WIDTH
IN_MAIN = COL_LR
LANES = 128

VMEM_LIMIT = 56 * 1024 * 1024

F32 = jnp.float32
BF16 = jnp.bfloat16

_NT = (((1,), (1,)), ((), ()))
_TN = (((0,), (0,)), ((), ()))


def _params(*sem):
    return pltpu.CompilerParams(dimension_semantics=sem, vmem_limit_bytes=VMEM_LIMIT)


def _rms(x, g):
    return x * lax.rsqrt(jnp.mean(x * x, axis=-1, keepdims=True) + EPS) * g


def _silu(x):
    return x * (1.0 / (1.0 + jnp.exp(-x)))


def _norm_kernel(x_ref, g_ref, o_ref):
    o_ref[...] = _rms(x_ref[...], g_ref[...]).astype(o_ref.dtype)


def _entry_norm(x, g, bm):
    m = x.shape[0]
    return pl.pallas_call(
        _norm_kernel,
        grid=(m // bm,),
        in_specs=[pl.BlockSpec((bm, D_MODEL), lambda i: (i, 0)),
                  pl.BlockSpec((1, D_MODEL), lambda i: (0, 0))],
        out_specs=pl.BlockSpec((bm, D_MODEL), lambda i: (i, 0)),
        out_shape=jax.ShapeDtypeStruct((m, D_MODEL), BF16),
        compiler_params=_params("parallel"),
        name="entry_norm",
    )(x, g)


def _mm_kernel(x_ref, w_ref, o_ref):
    o_ref[...] = jnp.dot(x_ref[...], w_ref[...], preferred_element_type=F32).astype(o_ref.dtype)


def _in_proj(hn, w_main, layer, bm, bn):
    m = hn.shape[0]
    return pl.pallas_call(
        _mm_kernel,
        grid=(m // bm, IN_MAIN // bn),
        in_specs=[pl.BlockSpec((bm, D_MODEL), lambda i, j: (i, 0)),
                  pl.BlockSpec((None, D_MODEL, bn), lambda i, j: (layer, 0, j))],
        out_specs=pl.BlockSpec((bm, bn), lambda i, j: (i, j)),
        out_shape=jax.ShapeDtypeStruct((m, IN_MAIN), BF16),
        compiler_params=_params("parallel", "arbitrary"),
        name="in_proj",
    )(hn, w_main)


def _gate_kernel(x_ref, wlr_ref, w2_ref, b2_ref, o_ref):
    lr = jnp.dot(x_ref[...], wlr_ref[...], preferred_element_type=F32)
    z = jnp.dot(lr.astype(BF16), w2_ref[...], preferred_element_type=F32) + b2_ref[...]
    log_sig = jnp.minimum(z, 0.0) - jnp.log1p(jnp.exp(-jnp.abs(z)))
    o_ref[...] = log_sig * (1.0 / GLA_GATE_NORM)


def _gate(hn, w_lr, w2, b2, layer, bm):
    m = hn.shape[0]
    return pl.pallas_call(
        _gate_kernel,
        grid=(m // bm,),
        in_specs=[pl.BlockSpec((bm, D_MODEL), lambda i: (i, 0)),
                  pl.BlockSpec((None, D_MODEL, LANES), lambda i: (layer, 0, 0)),
                  pl.BlockSpec((None, LANES, GLA_KEY_WIDTH), lambda i: (layer, 0, 0)),
                  pl.BlockSpec((None, 1, GLA_KEY_WIDTH), lambda i: (layer, 0, 0))],
        out_specs=pl.BlockSpec((bm, GLA_KEY_WIDTH), lambda i: (i, 0)),
        out_shape=jax.ShapeDtypeStruct((m, GLA_KEY_WIDTH), F32),
        compiler_params=_params("parallel"),
        name="gla_gate",
    )(hn, w_lr, w2, b2)


def _da_kernel(lam_ref, g_ref, q_ref, k_ref, v_ref, kp_ref, vp_ref, o_ref, m_scr, l_scr, acc_scr,
               *, tq, lam_init, has_prefix):
    qi = pl.program_id(2)
    lane = lax.broadcasted_iota(jnp.int32, (1, DA_HEAD_WIDTH), 1)
    q = q_ref[...]
    zero = jnp.zeros_like(q)
    scale = DA_HEAD_DIM ** -0.5
    qs = (jnp.where(lane < DA_HEAD_DIM, q, zero) * scale,
          jnp.where(lane >= DA_HEAD_DIM, q, zero) * scale)

    for c in range(2):
        if has_prefix:
            s = lax.dot_general(qs[c], kp_ref[...], _NT, preferred_element_type=F32)
            m = jnp.max(s, axis=-1, keepdims=True)
            p = jnp.exp(s - m)
            m_scr[c] = jnp.broadcast_to(m, (tq, LANES))
            l_scr[c] = jnp.broadcast_to(jnp.sum(p, axis=-1, keepdims=True), (tq, LANES))
            acc_scr[c] = jnp.dot(p.astype(BF16), vp_ref[...], preferred_element_type=F32)
        else:
            m_scr[c] = jnp.full((tq, LANES), -jnp.inf, F32)
            l_scr[c] = jnp.zeros((tq, LANES), F32)
            acc_scr[c] = jnp.zeros((tq, DA_HEAD_WIDTH), F32)

    def update(c, kblk, vblk, mask):
        s = lax.dot_general(qs[c], kblk, _NT, preferred_element_type=F32)
        if mask is not None:
            s = jnp.where(mask, s, -jnp.inf)
        m_prev = m_scr[c]
        m_new = jnp.maximum(m_prev, jnp.max(s, axis=-1, keepdims=True))
        alpha = jnp.exp(m_prev - m_new)
        p = jnp.exp(s - m_new[:, :1])
        l_scr[c] = alpha * l_scr[c] + jnp.sum(p, axis=-1, keepdims=True)
        acc_scr[c] = alpha * acc_scr[c] + jnp.dot(p.astype(BF16), vblk, preferred_element_type=F32)
        m_scr[c] = m_new

    def below_diagonal(j, carry):
        start = pl.multiple_of(j * tq, tq)
        kblk = k_ref[pl.ds(start, tq), :]
        vblk = v_ref[pl.ds(start, tq), :]
        update(0, kblk, vblk, None)
        update(1, kblk, vblk, None)
        return carry

    lax.fori_loop(0, qi, below_diagonal, 0)

    start = pl.multiple_of(qi * tq, tq)
    kblk = k_ref[pl.ds(start, tq), :]
    vblk = v_ref[pl.ds(start, tq), :]
    causal = (lax.broadcasted_iota(jnp.int32, (tq, tq), 0) >= lax.broadcasted_iota(jnp.int32, (tq, tq), 1))
    update(0, kblk, vblk, causal)
    update(1, kblk, vblk, causal)

    lamv = lam_ref[...]
    lam = (jnp.exp(jnp.sum(lamv[0:1] * lamv[1:2], axis=-1, keepdims=True))
           - jnp.exp(jnp.sum(lamv[2:3] * lamv[3:4], axis=-1, keepdims=True)) + lam_init)
    o = acc_scr[0] / l_scr[0] - lam * (acc_scr[1] / l_scr[1])
    o_ref[...] = (_rms(o, g_ref[...]) * (1.0 - lam_init)).astype(o_ref.dtype)


def _diff_attention(p, p_prefix, lam, subln_g, layer, lam_init, nbatch, seq, tq, has_prefix):
    nq = seq // tq
    kcol = COL_K_DA // DA_HEAD_WIDTH
    vcol = COL_V_DA // DA_HEAD_WIDTH
    kernel = functools.partial(_da_kernel, tq=tq, lam_init=lam_init, has_prefix=has_prefix)
    return pl.pallas_call(
        kernel,
        grid=(nbatch, DA_HEADS, nq),
        in_specs=[pl.BlockSpec((None, 4, DA_HEAD_DIM), lambda b, h, i: (layer, 0, 0)),
                  pl.BlockSpec((None, 1, DA_HEAD_WIDTH), lambda b, h, i: (layer, 0, 0)),
                  pl.BlockSpec((tq, DA_HEAD_WIDTH), lambda b, h, i: (b * nq + i, h)),
                  pl.BlockSpec((seq, DA_HEAD_WIDTH), lambda b, h, i: (b, kcol + h)),
                  pl.BlockSpec((seq, DA_HEAD_WIDTH), lambda b, h, i: (b, vcol + h)),
                  pl.BlockSpec((N_META, DA_HEAD_WIDTH), lambda b, h, i: (0, kcol + h)),
                  pl.BlockSpec((N_META, DA_HEAD_WIDTH), lambda b, h, i: (0, vcol + h))],
        out_specs=pl.BlockSpec((tq, DA_HEAD_WIDTH), lambda b, h, i: (b * nq + i, h)),
        out_shape=jax.ShapeDtypeStruct((nbatch * seq, DA_WIDTH), BF16),
        scratch_shapes=[pltpu.VMEM((2, tq, LANES), F32),
                        pltpu.VMEM((2, tq, LANES), F32),
                        pltpu.VMEM((2, tq, DA_HEAD_WIDTH), F32)],
        compiler_params=_params("parallel", "parallel", "arbitrary"),
        name="diff_attention",
    )(lam, subln_g, p, p, p, p_prefix, p_prefix)


def _split_bf16(x):
    hi = x.astype(BF16)
    lo = (x - hi.astype(F32)).astype(BF16)
    return hi, lo


def _gla_kernel(q_ref, k_ref, v_ref, go_ref, gk_ref, ng_ref, s0_ref, o_ref, sfin_ref, s_scr,
                *, chunk, nchunks):
    blk = pl.program_id(2)

    @pl.when(blk == 0)
    def _():
        s_scr[...] = s0_ref[...]

    row = lax.broadcasted_iota(jnp.int32, (chunk, chunk), 0)
    col = lax.broadcasted_iota(jnp.int32, (chunk, chunk), 1)
    tril = row >= col
    tril_bf = tril.astype(F32).astype(BF16)
    ones = jnp.ones((chunk, GLA_HEAD_V), BF16)
    ng = ng_ref[...]

    for c in range(nchunks):
        sl = pl.ds(c * chunk, chunk)
        gk_hi, gk_lo = _split_bf16(gk_ref[sl, :])
        b = (jnp.dot(tril_bf, gk_hi, preferred_element_type=F32)
             + jnp.dot(tril_bf, gk_lo, preferred_element_type=F32))
        b_last_col = (lax.dot_general(gk_hi, ones, _TN, preferred_element_type=F32)
                      + lax.dot_general(gk_lo, ones, _TN, preferred_element_type=F32))
        b_last = b[chunk - 1:chunk, :]
        q = q_ref[sl, :].astype(F32) * (GLA_HEAD_K ** -0.5)
        k = k_ref[sl, :].astype(F32)
        v = v_ref[sl, :]
        q_in = (q * jnp.exp(b)).astype(BF16)
        k_in = (k * jnp.exp(-b)).astype(BF16)
        k_dec = (k * jnp.exp(b_last - b)).astype(BF16)
        a = lax.dot_general(q_in, k_in, _NT, preferred_element_type=F32)
        a = jnp.where(tril, a, 0.0)
        s_prev = s_scr[...]
        o = (jnp.dot(a.astype(BF16), v, preferred_element_type=F32)
             + jnp.dot(q_in, s_prev.astype(BF16), preferred_element_type=F32))
        s_scr[...] = jnp.exp(b_last_col) * s_prev + lax.dot_general(k_dec, v, _TN, preferred_element_type=F32)
        go = go_ref[sl, :].astype(F32)
        o_ref[sl, :] = (_rms(o, ng) * _silu(go)).astype(o_ref.dtype)

    @pl.when(blk == pl.num_programs(2) - 1)
    def _():
        sfin_ref[...] = s_scr[...]


def _gla(p, gk, norm_g, s0, layer, nbatch, seq, chunk, nchunks):
    blk_rows = chunk * nchunks
    nblk = seq // blk_rows
    qcol = COL_Q_G // GLA_HEAD_K
    kcol = COL_K_G // GLA_HEAD_K
    vcol = COL_V_G // GLA_HEAD_V
    gcol = COL_G_G // GLA_HEAD_V
    kernel = functools.partial(_gla_kernel, chunk=chunk, nchunks=nchunks)
    return pl.pallas_call(
        kernel,
        grid=(nbatch, GLA_HEADS, nblk),
        in_specs=[pl.BlockSpec((blk_rows, GLA_HEAD_K), lambda b, h, i: (b * nblk + i, qcol + h)),
                  pl.BlockSpec((blk_rows, GLA_HEAD_K), lambda b, h, i: (b * nblk + i, kcol + h)),
                  pl.BlockSpec((blk_rows, GLA_HEAD_V), lambda b, h, i: (b * nblk + i, vcol + h)),
                  pl.BlockSpec((blk_rows, GLA_HEAD_V), lambda b, h, i: (b * nblk + i, gcol + h)),
                  pl.BlockSpec((blk_rows, GLA_HEAD_K), lambda b, h, i: (b * nblk + i, h)),
                  pl.BlockSpec((None, 1, GLA_HEAD_V), lambda b, h, i: (layer, 0, 0)),
                  pl.BlockSpec((None, GLA_HEAD_K, GLA_HEAD_V), lambda b, h, i: (h, 0, 0))],
        out_specs=[pl.BlockSpec((blk_rows, GLA_HEAD_V), lambda b, h, i: (b * nblk + i, h)),
                   pl.BlockSpec((None, GLA_HEAD_K, GLA_HEAD_V), lambda b, h, i: (b * GLA_HEADS + h, 0, 0))],
        out_shape=[jax.ShapeDtypeStruct((nbatch * seq, GLA_WIDTH), BF16),
                   jax.ShapeDtypeStruct((nbatch * GLA_HEADS, GLA_HEAD_K, GLA_HEAD_V), F32)],
        scratch_shapes=[pltpu.VMEM((GLA_HEAD_K, GLA_HEAD_V), F32)],
        compiler_params=_params("parallel", "parallel", "arbitrary"),
        name="gla",
    )(p, p, p, p, gk, norm_g, s0)


def _residual_epilogue(y, h_ref, gpost_ref, gnext_ref, ho_ref, hn_ref):
    h = h_ref[...] + _rms(y, gpost_ref[...])
    ho_ref[...] = h
    hn_ref[...] = _rms(h, gnext_ref[...]).astype(hn_ref.dtype)


def _out_proj_kernel(xa_ref, xb_ref, w_ref, h_ref, gpost_ref, gnext_ref, ho_ref, hn_ref):
    y = (jnp.dot(xa_ref[...], w_ref[:DA_WIDTH, :], preferred_element_type=F32)
         + jnp.dot(xb_ref[...], w_ref[DA_WIDTH:, :], preferred_element_type=F32))
    _residual_epilogue(y, h_ref, gpost_ref, gnext_ref, ho_ref, hn_ref)


def _out_proj(o_da, o_gla, w_out, h, g_post, g_next, layer, bm):
    m = h.shape[0]
    row = lambda i: (i, 0)
    return pl.pallas_call(
        _out_proj_kernel,
        grid=(m // bm,),
        in_specs=[pl.BlockSpec((bm, DA_WIDTH), row),
                  pl.BlockSpec((bm, GLA_WIDTH), row),
                  pl.BlockSpec((None, D_MODEL, D_MODEL), lambda i: (layer, 0, 0)),
                  pl.BlockSpec((bm, D_MODEL), row),
                  pl.BlockSpec((1, D_MODEL), lambda i: (0, 0)),
                  pl.BlockSpec((1, D_MODEL), lambda i: (0, 0))],
        out_specs=[pl.BlockSpec((bm, D_MODEL), row), pl.BlockSpec((bm, D_MODEL), row)],
        out_shape=[jax.ShapeDtypeStruct((m, D_MODEL), F32), jax.ShapeDtypeStruct((m, D_MODEL), BF16)],
        compiler_params=_params("parallel"),
        name="out_proj",
    )(o_da, o_gla, w_out, h, g_post, g_next)


def _down_proj_kernel(x_ref, w_ref, h_ref, gpost_ref, gnext_ref, ho_ref, hn_ref, acc_scr):
    kk = pl.program_id(1)

    @pl.when(kk == 0)
    def _():
        acc_scr[...] = jnp.zeros_like(acc_scr)

    acc_scr[...] += jnp.dot(x_ref[...], w_ref[...], preferred_element_type=F32)

    @pl.when(kk == pl.num_programs(1) - 1)
    def _():
        _residual_epilogue(acc_scr[...], h_ref, gpost_ref, gnext_ref, ho_ref, hn_ref)


def _down_proj(x, w_down, h, g_post, g_next, layer, bm, bk):
    m = h.shape[0]
    row = lambda i, k: (i, 0)
    return pl.pallas_call(
        _down_proj_kernel,
        grid=(m // bm, D_FF // bk),
        in_specs=[pl.BlockSpec((bm, bk), lambda i, k: (i, k)),
                  pl.BlockSpec((None, bk, D_MODEL), lambda i, k: (layer, k, 0)),
                  pl.BlockSpec((bm, D_MODEL), row),
                  pl.BlockSpec((1, D_MODEL), lambda i, k: (0, 0)),
                  pl.BlockSpec((1, D_MODEL), lambda i, k: (0, 0))],
        out_specs=[pl.BlockSpec((bm, D_MODEL), row), pl.BlockSpec((bm, D_MODEL), row)],
        out_shape=[jax.ShapeDtypeStruct((m, D_MODEL), F32), jax.ShapeDtypeStruct((m, D_MODEL), BF16)],
        scratch_shapes=[pltpu.VMEM((bm, D_MODEL), F32)],
        compiler_params=_params("parallel", "arbitrary"),
        name="down_proj",
    )(x, w_down, h, g_post, g_next)


_HALO = 8


def _up_conv_kernel(x_ref, wa_ref, wv_ref, cwa_ref, cwv_ref, cba_ref, cbv_ref, ha_ref, hv_ref, *rest,
                    bm, blocks_per_seq, emit_u):
    if emit_u:
        o_ref, ua_ref, uv_ref, ubuf = rest
    else:
        o_ref, ubuf = rest
    i = pl.program_id(1)
    x = x_ref[...]
    first = (i % blocks_per_seq) == 0
    conv = []
    for idx, (w_ref, cw_ref, cb_ref, h_ref) in enumerate(
            ((wa_ref, cwa_ref, cba_ref, ha_ref), (wv_ref, cwv_ref, cbv_ref, hv_ref))):
        u = jnp.dot(x, w_ref[...], preferred_element_type=F32)
        buf = ubuf.at[idx]

        @pl.when(first)
        def _():
            buf[_HALO - 2:_HALO, :] = h_ref[N_META - 2:N_META, :]

        @pl.when(jnp.logical_not(first))
        def _():
            buf[_HALO - 2:_HALO, :] = buf[_HALO + bm - 2:_HALO + bm, :]

        buf[_HALO:_HALO + bm, :] = u
        cw = cw_ref[...]
        conv.append(cb_ref[...]
                    + cw[0:1] * buf[_HALO - 2:_HALO - 2 + bm, :]
                    + cw[1:2] * buf[_HALO - 1:_HALO - 1 + bm, :]
                    + cw[2:3] * u)
        if emit_u:
            (ua_ref, uv_ref)[idx][...] = u
    o_ref[...] = (_silu(conv[0]) * conv[1]).astype(o_ref.dtype)


def _up_conv(hn, w_up, conv_w, conv_b, hist_a, hist_v, layer, bm, bn, blocks_per_seq, emit_u):
    m = hn.shape[0]
    nj = D_FF // bn
    kernel = functools.partial(_up_conv_kernel, bm=bm, blocks_per_seq=blocks_per_seq, emit_u=emit_u)
    out_specs = [pl.BlockSpec((bm, bn), lambda j, i: (i, j))]
    out_shape = [jax.ShapeDtypeStruct((m, D_FF), BF16)]
    if emit_u:
        out_specs += [pl.BlockSpec((bm, bn), lambda j, i: (i, j))] * 2
        out_shape += [jax.ShapeDtypeStruct((m, D_FF), F32)] * 2
    return pl.pallas_call(
        kernel,
        grid=(nj, m // bm),
        in_specs=[pl.BlockSpec((bm, D_MODEL), lambda j, i: (i, 0)),
                  pl.BlockSpec((None, D_MODEL, bn), lambda j, i: (layer, 0, j)),
                  pl.BlockSpec((None, D_MODEL, bn), lambda j, i: (layer, 0, j + nj)),
                  pl.BlockSpec((None, 3, bn), lambda j, i: (layer, 0, j)),
                  pl.BlockSpec((None, 3, bn), lambda j, i: (layer, 0, j + nj)),
                  pl.BlockSpec((None, 1, bn), lambda j, i: (layer, 0, j)),
                  pl.BlockSpec((None, 1, bn), lambda j, i: (layer, 0, j + nj)),
                  pl.BlockSpec((N_META, bn), lambda j, i: (0, j)),
                  pl.BlockSpec((N_META, bn), lambda j, i: (0, j))],
        out_specs=out_specs,
        out_shape=out_shape,
        scratch_shapes=[pltpu.VMEM((2, _HALO + bm, bn), F32)],
        compiler_params=_params("parallel", "arbitrary"),
        name="up_conv",
    )(hn, w_up, w_up, conv_w, conv_w, conv_b, conv_b, hist_a, hist_v)


def _lam_init(layer):
    return 0.8 - 0.6 * math.exp(-0.3 * layer)


def _trunk(h, prm, side_in, *, nbatch, seq, cfg):
    is_meta = side_in is None
    hn = _entry_norm(h, prm["pre_mix_g"][0], cfg["bm_norm"])
    side_out = []
    for l in range(DEPTH):
        lam_init = _lam_init(l)
        p = _in_proj(hn, prm["w_in"], l, cfg["bm_in"], cfg["bn_in"])
        gk = _gate(hn, prm["w_lr"], prm["w2"], prm["b2"], l, cfg["bm_gate"])
        if is_meta:
            p_prefix = p
            s0 = jnp.zeros((GLA_HEADS, GLA_HEAD_K, GLA_HEAD_V), F32)
            hist_a = hist_v = jnp.zeros((N_META, D_FF), F32)
        else:
            p_prefix, s0, hist_a, hist_v = side_in[l]
        o_da = _diff_attention(p, p_prefix, prm["da_lambda"], prm["da_subln_g"], l, lam_init,
                               nbatch, seq, cfg["tq"], has_prefix=not is_meta)
        o_gla, s_fin = _gla(p, gk, prm["gla_norm_g"], s0, l, nbatch, seq, cfg["chunk"], cfg["nchunks"])
        h, hn = _out_proj(o_da, o_gla, prm["w_out"], h, prm["post_mix_g"][l], prm["pre_ffn_g"][l], l, cfg["bm_out"])
        ffn = _up_conv(hn, prm["w_up"], prm["conv_w"], prm["conv_b"], hist_a, hist_v, l,
                       cfg["bm_up"], cfg["bn_up"], seq // cfg["bm_up"], emit_u=is_meta)
        if is_meta:
            act, u_a, u_v = ffn
            side_out.append((p, s_fin, u_a, u_v))
        else:
            (act,) = ffn
        g_next = prm["pre_mix_g"][(l + 1) % DEPTH]
        h, hn = _down_proj(act, prm["w_down"], h, prm["post_ffn_g"][l], g_next, l, cfg["bm_down"], cfg["bk_down"])
    return h, side_out


_META_CFG = dict(bm_norm=N_META, bm_in=N_META, bn_in=1024, bm_gate=N_META, tq=N_META, chunk=N_META, nchunks=1,
                 bm_out=N_META, bm_up=N_META, bn_up=512, bm_down=N_META, bk_down=1408)
_SEQ_CFG = dict(bm_norm=512, bm_in=1024, bn_in=512, bm_gate=1024, tq=256, chunk=GLA_CHUNK, nchunks=4,
                bm_out=512, bm_up=1024, bn_up=512, bm_down=512, bk_down=1408)


def kernel(x, meta_tokens, pre_mix_g, w_in, da_lambda, da_subln_g, gla_gate_w2, gla_gate_b, gla_norm_g, w_out,
           post_mix_g, pre_ffn_g, w_up, conv_w, conv_b, w_down, post_ffn_g):
    vec = lambda a: a.astype(F32).reshape(DEPTH, 1, a.shape[-1])
    prm = dict(
        pre_mix_g=vec(pre_mix_g), post_mix_g=vec(post_mix_g), pre_ffn_g=vec(pre_ffn_g), post_ffn_g=vec(post_ffn_g),
        w_in=w_in[:, :, :IN_MAIN].astype(BF16),
        w_lr=jnp.pad(w_in[:, :, IN_MAIN:], ((0, 0), (0, 0), (0, LANES - GLA_GATE_RANK))).astype(BF16),
        w2=jnp.pad(gla_gate_w2, ((0, 0), (0, LANES - GLA_GATE_RANK), (0, 0))).astype(BF16),
        b2=vec(gla_gate_b),
        da_lambda=da_lambda.astype(F32), da_subln_g=vec(da_subln_g), gla_norm_g=vec(gla_norm_g),
        w_out=w_out.astype(BF16), w_up=w_up.astype(BF16), w_down=w_down.astype(BF16),
        conv_w=conv_w.astype(F32), conv_b=vec(conv_b),
    )
    _, side = _trunk(meta_tokens.astype(F32), prm, None, nbatch=1, seq=N_META, cfg=_META_CFG)
    h, _ = _trunk(x.astype(F32).reshape(BATCH * SEQ, D_MODEL), prm, side, nbatch=BATCH, seq=SEQ, cfg=_SEQ_CFG)
    return h.reshape(BATCH, SEQ, D_MODEL).astype(x.dtype)
```

```python
import functools
import math

import jax
import jax.numpy as jnp
from jax import lax
from jax.experimental import pallas as pl
from jax.experimental.pallas import tpu as pltpu

D_MODEL = 2048
BATCH = 4
SEQ = 2048
DEPTH = 4
N_META = 16

DA_HEADS = 8
DA_HEAD_DIM = 64
DA_HEAD_WIDTH = 2 * DA_HEAD_DIM
DA_WIDTH = DA_HEADS * DA_HEAD_WIDTH
GLA_HEADS = 4
GLA_WIDTH = 1024
GLA_HEAD_V = 256
GLA_KEY_WIDTH = 512
GLA_HEAD_K = 128
GLA_GATE_RANK = 16
GLA_GATE_NORM = 16.0
GLA_CHUNK = 64
D_FF = 5632
EPS = 1e-6

COL_Q_DA = 0
COL_K_DA = DA_WIDTH
COL_V_DA = 2 * DA_WIDTH
COL_Q_G = 3 * DA_WIDTH
COL_K_G = COL_Q_G + GLA_KEY_WIDTH
COL_V_G = COL_K_G + GLA_KEY_WIDTH
COL_G_G = COL_V_G + GLA_WIDTH
COL_LR = COL_G_G + GLA_WIDTH
IN_MAIN = COL_LR
LANES = 128

VMEM_LIMIT = 56 * 1024 * 1024

F32 = jnp.float32
BF16 = jnp.bfloat16

_NT = (((1,), (1,)), ((), ()))
_TN = (((0,), (0,)), ((), ()))
_LOG2E = math.log2(math.e)


def _params(*sem):
    return pltpu.CompilerParams(dimension_semantics=sem, vmem_limit_bytes=VMEM_LIMIT)


def _rms(x, g):
    return x * lax.rsqrt(jnp.mean(x * x, axis=-1, keepdims=True) + EPS) * g


def _silu(x):
    return x * (1.0 / (1.0 + jnp.exp(-x)))


def _norm_kernel(x_ref, g_ref, o_ref):
    o_ref[...] = _rms(x_ref[...], g_ref[...]).astype(o_ref.dtype)


def _entry_norm(x, g, bm):
    m = x.shape[0]
    return pl.pallas_call(
        _norm_kernel,
        grid=(m // bm,),
        in_specs=[pl.BlockSpec((bm, D_MODEL), lambda i: (i, 0)),
                  pl.BlockSpec((1, D_MODEL), lambda i: (0, 0))],
        out_specs=pl.BlockSpec((bm, D_MODEL), lambda i: (i, 0)),
        out_shape=jax.ShapeDtypeStruct((m, D_MODEL), BF16),
        compiler_params=_params("parallel"),
        name="entry_norm",
    )(x, g)


def _mm_kernel(x_ref, w_ref, o_ref):
    o_ref[...] = jnp.dot(x_ref[...], w_ref[...], preferred_element_type=F32).astype(o_ref.dtype)


def _in_proj(hn, w_main, layer, bm, bn):
    m = hn.shape[0]
    return pl.pallas_call(
        _mm_kernel,
        grid=(m // bm, IN_MAIN // bn),
        in_specs=[pl.BlockSpec((bm, D_MODEL), lambda i, j: (i, 0)),
                  pl.BlockSpec((None, D_MODEL, bn), lambda i, j: (layer, 0, j))],
        out_specs=pl.BlockSpec((bm, bn), lambda i, j: (i, j)),
        out_shape=jax.ShapeDtypeStruct((m, IN_MAIN), BF16),
        compiler_params=_params("parallel", "arbitrary"),
        name="in_proj",
    )(hn, w_main)


def _gate_kernel(x_ref, wlr_ref, w2_ref, b2_ref, o_ref):
    lr = jnp.dot(x_ref[...], wlr_ref[...], preferred_element_type=F32)
    z = jnp.dot(lr.astype(BF16), w2_ref[...], preferred_element_type=F32) + b2_ref[...]
    log_sig = jnp.minimum(z, 0.0) - jnp.log1p(jnp.exp(-jnp.abs(z)))
    o_ref[...] = log_sig * (1.0 / GLA_GATE_NORM)


def _gate(hn, w_lr, w2, b2, layer, bm):
    m = hn.shape[0]
    return pl.pallas_call(
        _gate_kernel,
        grid=(m // bm,),
        in_specs=[pl.BlockSpec((bm, D_MODEL), lambda i: (i, 0)),
                  pl.BlockSpec((None, D_MODEL, LANES), lambda i: (layer, 0, 0)),
                  pl.BlockSpec((None, LANES, GLA_KEY_WIDTH), lambda i: (layer, 0, 0)),
                  pl.BlockSpec((None, 1, GLA_KEY_WIDTH), lambda i: (layer, 0, 0))],
        out_specs=pl.BlockSpec((bm, GLA_KEY_WIDTH), lambda i: (i, 0)),
        out_shape=jax.ShapeDtypeStruct((m, GLA_KEY_WIDTH), F32),
        compiler_params=_params("parallel"),
        name="gla_gate",
    )(hn, w_lr, w2, b2)


def _da_kernel(lam_ref, g_ref, q_ref, k_ref, v_ref, kp_ref, vp_ref, o_ref, m_scr, acc_scr, vx_scr, vpx_scr,
               *, tq, lam_init, has_prefix):
    qi = pl.program_id(2)
    seq = v_ref.shape[0]
    hw = DA_HEAD_WIDTH

    @pl.when(qi == 0)
    def _():
        vx_scr[:, :hw] = v_ref[...]
        vx_scr[:, hw:] = jnp.ones((seq, hw), BF16)
        if has_prefix:
            vpx_scr[:, :hw] = vp_ref[...]
            vpx_scr[:, hw:] = jnp.ones((N_META, hw), BF16)

    lane = lax.broadcasted_iota(jnp.int32, (1, hw), 1)
    q = q_ref[...].astype(F32) * (DA_HEAD_DIM ** -0.5 * _LOG2E)
    qs = (jnp.where(lane < DA_HEAD_DIM, q, 0.0).astype(BF16),
          jnp.where(lane >= DA_HEAD_DIM, q, 0.0).astype(BF16))

    m_scr[...] = jnp.full(m_scr.shape, -jnp.inf, F32)
    acc_scr[...] = jnp.zeros(acc_scr.shape, F32)

    def lanes_of(col, width):
        if width % LANES == 0:
            return col if width == LANES else jnp.concatenate([col] * (width // LANES), axis=1)
        return col[:, :1]

    def accumulate(c, scores, values):
        m_prev = m_scr[c]
        m_cur = functools.reduce(jnp.maximum, [jnp.max(s, axis=-1, keepdims=True) for s in scores])
        m_new = jnp.maximum(m_prev, m_cur)
        alpha = jnp.exp2(m_prev - m_new)
        pv = None
        for s, vx in zip(scores, values):
            p = jnp.exp2(s - lanes_of(m_new, s.shape[1])).astype(BF16)
            t = jnp.dot(p, vx, preferred_element_type=F32)
            pv = t if pv is None else pv + t
        acc_scr[c, :, :hw] = alpha * acc_scr[c, :, :hw] + pv[:, :hw]
        acc_scr[c, :, hw:] = alpha * acc_scr[c, :, hw:] + pv[:, hw:]
        m_scr[c] = m_new

    def below_diagonal(j, carry):
        start = pl.multiple_of(j * tq, tq)
        kblk = k_ref[pl.ds(start, tq), :]
        vx = vx_scr[pl.ds(start, tq), :]
        for c in range(2):
            accumulate(c, [lax.dot_general(qs[c], kblk, _NT, preferred_element_type=F32)], [vx])
        return carry

    lax.fori_loop(0, qi, below_diagonal, 0)

    start = pl.multiple_of(qi * tq, tq)
    kblk = k_ref[pl.ds(start, tq), :]
    vx = vx_scr[pl.ds(start, tq), :]
    causal = (lax.broadcasted_iota(jnp.int32, (tq, tq), 0) >= lax.broadcasted_iota(jnp.int32, (tq, tq), 1))
    for c in range(2):
        scores = [jnp.where(causal, lax.dot_general(qs[c], kblk, _NT, preferred_element_type=F32), -jnp.inf)]
        values = [vx]
        if has_prefix:
            scores.append(lax.dot_general(qs[c], kp_ref[...], _NT, preferred_element_type=F32))
            values.append(vpx_scr[...])
        accumulate(c, scores, values)

    lamv = lam_ref[...]
    lam = (jnp.exp(jnp.sum(lamv[0:1] * lamv[1:2], axis=-1, keepdims=True))
           - jnp.exp(jnp.sum(lamv[2:3] * lamv[3:4], axis=-1, keepdims=True)) + lam_init)
    o = (acc_scr[0, :, :hw] / acc_scr[0, :, hw:]) - lam * (acc_scr[1, :, :hw] / acc_scr[1, :, hw:])
    o_ref[...] = (_rms(o, g_ref[...]) * (1.0 - lam_init)).astype(o_ref.dtype)


def _diff_attention(p, p_prefix, lam, subln_g, layer, lam_init, nbatch, seq, tq, has_prefix):
    nq = seq // tq
    kcol = COL_K_DA // DA_HEAD_WIDTH
    vcol = COL_V_DA // DA_HEAD_WIDTH
    kernel = functools.partial(_da_kernel, tq=tq, lam_init=lam_init, has_prefix=has_prefix)
    return pl.pallas_call(
        kernel,
        grid=(nbatch, DA_HEADS, nq),
        in_specs=[pl.BlockSpec((None, 4, DA_HEAD_DIM), lambda b, h, i: (layer, 0, 0)),
                  pl.BlockSpec((None, 1, DA_HEAD_WIDTH), lambda b, h, i: (layer, 0, 0)),
                  pl.BlockSpec((tq, DA_HEAD_WIDTH), lambda b, h, i: (b * nq + i, h)),
                  pl.BlockSpec((seq, DA_HEAD_WIDTH), lambda b, h, i: (b, kcol + h)),
                  pl.BlockSpec((seq, DA_HEAD_WIDTH), lambda b, h, i: (b, vcol + h)),
                  pl.BlockSpec((N_META, DA_HEAD_WIDTH), lambda b, h, i: (0, kcol + h)),
                  pl.BlockSpec((N_META, DA_HEAD_WIDTH), lambda b, h, i: (0, vcol + h))],
        out_specs=pl.BlockSpec((tq, DA_HEAD_WIDTH), lambda b, h, i: (b * nq + i, h)),
        out_shape=jax.ShapeDtypeStruct((nbatch * seq, DA_WIDTH), BF16),
        scratch_shapes=[pltpu.VMEM((2, tq, LANES), F32),
                        pltpu.VMEM((2, tq, 2 * DA_HEAD_WIDTH), F32),
                        pltpu.VMEM((seq, 2 * DA_HEAD_WIDTH), BF16),
                        pltpu.VMEM((N_META, 2 * DA_HEAD_WIDTH), BF16)],
        compiler_params=_params("parallel", "parallel", "arbitrary"),
        name="diff_attention",
    )(lam, subln_g, p, p, p, p_prefix, p_prefix)


def _split_bf16(x):
    hi = x.astype(BF16)
    lo = (x - hi.astype(F32)).astype(BF16)
    return hi, lo


def _gla_kernel(q_ref, k_ref, v_ref, go_ref, gk_ref, ng_ref, s0_ref, o_ref, sfin_ref, s_scr,
                *, chunk, nchunks):
    blk = pl.program_id(2)

    @pl.when(blk == 0)
    def _():
        s_scr[...] = s0_ref[...]

    row = lax.broadcasted_iota(jnp.int32, (chunk, chunk), 0)
    col = lax.broadcasted_iota(jnp.int32, (chunk, chunk), 1)
    tril = row >= col
    tril_bf = tril.astype(F32).astype(BF16)
    ones = jnp.ones((chunk, GLA_HEAD_V), BF16)
    ng = ng_ref[...]

    for c in range(nchunks):
        sl = pl.ds(c * chunk, chunk)
        gk_hi, gk_lo = _split_bf16(gk_ref[sl, :])
        b = (jnp.dot(tril_bf, gk_hi, preferred_element_type=F32)
             + jnp.dot(tril_bf, gk_lo, preferred_element_type=F32))
        b_last_col = (lax.dot_general(gk_hi, ones, _TN, preferred_element_type=F32)
                      + lax.dot_general(gk_lo, ones, _TN, preferred_element_type=F32))
        b_last = b[chunk - 1:chunk, :]
        q = q_ref[sl, :].astype(F32) * (GLA_HEAD_K ** -0.5)
        k = k_ref[sl, :].astype(F32)
        v = v_ref[sl, :]
        q_in = (q * jnp.exp(b)).astype(BF16)
        k_in = (k * jnp.exp(-b)).astype(BF16)
        k_dec = (k * jnp.exp(b_last - b)).astype(BF16)
        a = lax.dot_general(q_in, k_in, _NT, preferred_element_type=F32)
        a = jnp.where(tril, a, 0.0)
        s_prev = s_scr[...]
        o = (jnp.dot(a.astype(BF16), v, preferred_element_type=F32)
             + jnp.dot(q_in, s_prev.astype(BF16), preferred_element_type=F32))
        s_scr[...] = jnp.exp(b_last_col) * s_prev + lax.dot_general(k_dec, v, _TN, preferred_element_type=F32)
        go = go_ref[sl, :].astype(F32)
        o_ref[sl, :] = (_rms(o, ng) * _silu(go)).astype(o_ref.dtype)

    @pl.when(blk == pl.num_programs(2) - 1)
    def _():
        sfin_ref[...] = s_scr[...]


def _gla(p, gk, norm_g, s0, layer, nbatch, seq, chunk, nchunks):
    blk_rows = chunk * nchunks
    nblk = seq // blk_rows
    qcol = COL_Q_G // GLA_HEAD_K
    kcol = COL_K_G // GLA_HEAD_K
    vcol = COL_V_G // GLA_HEAD_V
    gcol = COL_G_G // GLA_HEAD_V
    kernel = functools.partial(_gla_kernel, chunk=chunk, nchunks=nchunks)
    return pl.pallas_call(
        kernel,
        grid=(nbatch, GLA_HEADS, nblk),
        in_specs=[pl.BlockSpec((blk_rows, GLA_HEAD_K), lambda b, h, i: (b * nblk + i, qcol + h)),
                  pl.BlockSpec((blk_rows, GLA_HEAD_K), lambda b, h, i: (b * nblk + i, kcol + h)),
                  pl.BlockSpec((blk_rows, GLA_HEAD_V), lambda b, h, i: (b * nblk + i, vcol + h)),
                  pl.BlockSpec((blk_rows, GLA_HEAD_V), lambda b, h, i: (b * nblk + i, gcol + h)),
                  pl.BlockSpec((blk_rows, GLA_HEAD_K), lambda b, h, i: (b * nblk + i, h)),
                  pl.BlockSpec((None, 1, GLA_HEAD_V), lambda b, h, i: (layer, 0, 0)),
                  pl.BlockSpec((None, GLA_HEAD_K, GLA_HEAD_V), lambda b, h, i: (h, 0, 0))],
        out_specs=[pl.BlockSpec((blk_rows, GLA_HEAD_V), lambda b, h, i: (b * nblk + i, h)),
                   pl.BlockSpec((None, GLA_HEAD_K, GLA_HEAD_V), lambda b, h, i: (b * GLA_HEADS + h, 0, 0))],
        out_shape=[jax.ShapeDtypeStruct((nbatch * seq, GLA_WIDTH), BF16),
                   jax.ShapeDtypeStruct((nbatch * GLA_HEADS, GLA_HEAD_K, GLA_HEAD_V), F32)],
        scratch_shapes=[pltpu.VMEM((GLA_HEAD_K, GLA_HEAD_V), F32)],
        compiler_params=_params("parallel", "parallel", "arbitrary"),
        name="gla",
    )(p, p, p, p, gk, norm_g, s0)


def _residual_epilogue(y, h_ref, gpost_ref, gnext_ref, ho_ref, hn_ref):
    h = h_ref[...] + _rms(y, gpost_ref[...])
    ho_ref[...] = h
    hn_ref[...] = _rms(h, gnext_ref[...]).astype(hn_ref.dtype)


def _out_proj_kernel(xa_ref, xb_ref, w_ref, h_ref, gpost_ref, gnext_ref, ho_ref, hn_ref):
    y = (jnp.dot(xa_ref[...], w_ref[:DA_WIDTH, :], preferred_element_type=F32)
         + jnp.dot(xb_ref[...], w_ref[DA_WIDTH:, :], preferred_element_type=F32))
    _residual_epilogue(y, h_ref, gpost_ref, gnext_ref, ho_ref, hn_ref)


def _out_proj(o_da, o_gla, w_out, h, g_post, g_next, layer, bm):
    m = h.shape[0]
    row = lambda i: (i, 0)
    return pl.pallas_call(
        _out_proj_kernel,
        grid=(m // bm,),
        in_specs=[pl.BlockSpec((bm, DA_WIDTH), row),
                  pl.BlockSpec((bm, GLA_WIDTH), row),
                  pl.BlockSpec((None, D_MODEL, D_MODEL), lambda i: (layer, 0, 0)),
                  pl.BlockSpec((bm, D_MODEL), row),
                  pl.BlockSpec((1, D_MODEL), lambda i: (0, 0)),
                  pl.BlockSpec((1, D_MODEL), lambda i: (0, 0))],
        out_specs=[pl.BlockSpec((bm, D_MODEL), row), pl.BlockSpec((bm, D_MODEL), row)],
        out_shape=[jax.ShapeDtypeStruct((m, D_MODEL), F32), jax.ShapeDtypeStruct((m, D_MODEL), BF16)],
        compiler_params=_params("parallel"),
        name="out_proj",
    )(o_da, o_gla, w_out, h, g_post, g_next)


def _down_proj_kernel(x_ref, w_ref, h_ref, gpost_ref, gnext_ref, ho_ref, hn_ref, acc_scr):
    kk = pl.program_id(1)

    @pl.when(kk == 0)
    def _():
        acc_scr[...] = jnp.zeros_like(acc_scr)

    acc_scr[...] += jnp.dot(x_ref[...], w_ref[...], preferred_element_type=F32)

    @pl.when(kk == pl.num_programs(1) - 1)
    def _():
        _residual_epilogue(acc_scr[...], h_ref, gpost_ref, gnext_ref, ho_ref, hn_ref)


def _down_proj(x, w_down, h, g_post, g_next, layer, bm, bk):
    m = h.shape[0]
    row = lambda i, k: (i, 0)
    return pl.pallas_call(
        _down_proj_kernel,
        grid=(m // bm, D_FF // bk),
        in_specs=[pl.BlockSpec((bm, bk), lambda i, k: (i, k)),
                  pl.BlockSpec((None, bk, D_MODEL), lambda i, k: (layer, k, 0)),
                  pl.BlockSpec((bm, D_MODEL), row),
                  pl.BlockSpec((1, D_MODEL), lambda i, k: (0, 0)),
                  pl.BlockSpec((1, D_MODEL), lambda i, k: (0, 0))],
        out_specs=[pl.BlockSpec((bm, D_MODEL), row), pl.BlockSpec((bm, D_MODEL), row)],
        out_shape=[jax.ShapeDtypeStruct((m, D_MODEL), F32), jax.ShapeDtypeStruct((m, D_MODEL), BF16)],
        scratch_shapes=[pltpu.VMEM((bm, D_MODEL), F32)],
        compiler_params=_params("parallel", "arbitrary"),
        name="down_proj",
    )(x, w_down, h, g_post, g_next)


_HALO = 8
_UP_SUB_ROWS = 256


def _up_conv_kernel(x_ref, wa_ref, wv_ref, cwa_ref, cwv_ref, cba_ref, cbv_ref, ha_ref, hv_ref, *rest,
                    bm, sub_rows, blocks_per_seq, emit_u):
    if emit_u:
        o_ref, ua_ref, uv_ref, ubuf = rest
    else:
        o_ref, ubuf = rest
    i = pl.program_id(1)
    first = (i % blocks_per_seq) == 0
    halves = ((wa_ref, cwa_ref, cba_ref, ha_ref), (wv_ref, cwv_ref, cbv_ref, hv_ref))
    for idx, (_, _, _, h_ref) in enumerate(halves):
        buf = ubuf.at[idx]

        @pl.when(first)
        def _():
            buf[_HALO - 2:_HALO, :] = h_ref[N_META - 2:N_META, :]

        @pl.when(jnp.logical_not(first))
        def _():
            buf[_HALO - 2:_HALO, :] = buf[_HALO + bm - 2:_HALO + bm, :]

    sb = min(bm, sub_rows)
    for r in range(bm // sb):
        rows = slice(r * sb, (r + 1) * sb)
        lo = _HALO + r * sb
        x = x_ref[rows, :]
        conv = []
        for idx, (w_ref, cw_ref, cb_ref, _) in enumerate(halves):
            u = jnp.dot(x, w_ref[...], preferred_element_type=F32)
            buf = ubuf.at[idx]
            buf[lo:lo + sb, :] = u
            cw = cw_ref[...]
            conv.append(cb_ref[...]
                        + cw[0:1] * buf[lo - 2:lo - 2 + sb, :]
                        + cw[1:2] * buf[lo - 1:lo - 1 + sb, :]
                        + cw[2:3] * u)
            if emit_u:
                (ua_ref, uv_ref)[idx][rows, :] = u
        o_ref[rows, :] = (_silu(conv[0]) * conv[1]).astype(o_ref.dtype)


def _up_conv(hn, w_up, conv_w, conv_b, hist_a, hist_v, layer, bm, bn, blocks_per_seq, emit_u):
    m = hn.shape[0]
    nj = D_FF // bn
    kernel = functools.partial(_up_conv_kernel, bm=bm, sub_rows=_UP_SUB_ROWS, blocks_per_seq=blocks_per_seq,
                               emit_u=emit_u)
    out_specs = [pl.BlockSpec((bm, bn), lambda j, i: (i, j))]
    out_shape = [jax.ShapeDtypeStruct((m, D_FF), BF16)]
    if emit_u:
        out_specs += [pl.BlockSpec((bm, bn), lambda j, i: (i, j))] * 2
        out_shape += [jax.ShapeDtypeStruct((m, D_FF), F32)] * 2
    return pl.pallas_call(
        kernel,
        grid=(nj, m // bm),
        in_specs=[pl.BlockSpec((bm, D_MODEL), lambda j, i: (i, 0)),
                  pl.BlockSpec((None, D_MODEL, bn), lambda j, i: (layer, 0, j)),
                  pl.BlockSpec((None, D_MODEL, bn), lambda j, i: (layer, 0, j + nj)),
                  pl.BlockSpec((None, 3, bn), lambda j, i: (layer, 0, j)),
                  pl.BlockSpec((None, 3, bn), lambda j, i: (layer, 0, j + nj)),
                  pl.BlockSpec((None, 1, bn), lambda j, i: (layer, 0, j)),
                  pl.BlockSpec((None, 1, bn), lambda j, i: (layer, 0, j + nj)),
                  pl.BlockSpec((N_META, bn), lambda j, i: (0, j)),
                  pl.BlockSpec((N_META, bn), lambda j, i: (0, j))],
        out_specs=out_specs,
        out_shape=out_shape,
        scratch_shapes=[pltpu.VMEM((2, _HALO + bm, bn), F32)],
        compiler_params=_params("parallel", "arbitrary"),
        name="up_conv",
    )(hn, w_up, w_up, conv_w, conv_w, conv_b, conv_b, hist_a, hist_v)


def _lam_init(layer):
    return 0.8 - 0.6 * math.exp(-0.3 * layer)


def _trunk(h, prm, side_in, *, nbatch, seq, cfg):
    is_meta = side_in is None
    hn = _entry_norm(h, prm["pre_mix_g"][0], cfg["bm_norm"])
    side_out = []
    for l in range(DEPTH):
        lam_init = _lam_init(l)
        p = _in_proj(hn, prm["w_in"], l, cfg["bm_in"], cfg["bn_in"])
        gk = _gate(hn, prm["w_lr"], prm["w2"], prm["b2"], l, cfg["bm_gate"])
        if is_meta:
            p_prefix = p
            s0 = jnp.zeros((GLA_HEADS, GLA_HEAD_K, GLA_HEAD_V), F32)
            hist_a = hist_v = jnp.zeros((N_META, D_FF), F32)
        else:
            p_prefix, s0, hist_a, hist_v = side_in[l]
        o_da = _diff_attention(p, p_prefix, prm["da_lambda"], prm["da_subln_g"], l, lam_init,
                               nbatch, seq, cfg["tq"], has_prefix=not is_meta)
        o_gla, s_fin = _gla(p, gk, prm["gla_norm_g"], s0, l, nbatch, seq, cfg["chunk"], cfg["nchunks"])
        h, hn = _out_proj(o_da, o_gla, prm["w_out"], h, prm["post_mix_g"][l], prm["pre_ffn_g"][l], l, cfg["bm_out"])
        ffn = _up_conv(hn, prm["w_up"], prm["conv_w"], prm["conv_b"], hist_a, hist_v, l,
                       cfg["bm_up"], cfg["bn_up"], seq // cfg["bm_up"], emit_u=is_meta)
        if is_meta:
            act, u_a, u_v = ffn
            side_out.append((p, s_fin, u_a, u_v))
        else:
            (act,) = ffn
        g_next = prm["pre_mix_g"][(l + 1) % DEPTH]
        h, hn = _down_proj(act, prm["w_down"], h, prm["post_ffn_g"][l], g_next, l, cfg["bm_down"], cfg["bk_down"])
    return h, side_out


_META_CFG = dict(bm_norm=N_META, bm_in=N_META, bn_in=1024, bm_gate=N_META, tq=N_META, chunk=N_META, nchunks=1,
                 bm_out=N_META, bm_up=N_META, bn_up=512, bm_down=N_META, bk_down=1408)
_SEQ_CFG = dict(bm_norm=512, bm_in=1024, bn_in=512, bm_gate=1024, tq=512, chunk=GLA_CHUNK, nchunks=4,
                bm_out=512, bm_up=1024, bn_up=512, bm_down=512, bk_down=1408)


def kernel(x, meta_tokens, pre_mix_g, w_in, da_lambda, da_subln_g, gla_gate_w2, gla_gate_b, gla_norm_g, w_out,
           post_mix_g, pre_ffn_g, w_up, conv_w, conv_b, w_down, post_ffn_g):
    vec = lambda a: a.astype(F32).reshape(DEPTH, 1, a.shape[-1])
    prm = dict(
        pre_mix_g=vec(pre_mix_g), post_mix_g=vec(post_mix_g), pre_ffn_g=vec(pre_ffn_g), post_ffn_g=vec(post_ffn_g),
        w_in=w_in[:, :, :IN_MAIN].astype(BF16),
        w_lr=jnp.pad(w_in[:, :, IN_MAIN:], ((0, 0), (0, 0), (0, LANES - GLA_GATE_RANK))).astype(BF16),
        w2=jnp.pad(gla_gate_w2, ((0, 0), (0, LANES - GLA_GATE_RANK), (0, 0))).astype(BF16),
        b2=vec(gla_gate_b),
        da_lambda=da_lambda.astype(F32), da_subln_g=vec(da_subln_g), gla_norm_g=vec(gla_norm_g),
        w_out=w_out.astype(BF16), w_up=w_up.astype(BF16), w_down=w_down.astype(BF16),
        conv_w=conv_w.astype(F32), conv_b=vec(conv_b),
    )
    _, side = _trunk(meta_tokens.astype(F32), prm, None, nbatch=1, seq=N_META, cfg=_META_CFG)
    h, _ = _trunk(x.astype(F32).reshape(BATCH * SEQ, D_MODEL), prm, side, nbatch=BATCH, seq=SEQ, cfg=_SEQ_CFG)
    return h.reshape(BATCH, SEQ, D_MODEL).astype(x.dtype)
```

```python
import functools
import math

import jax
import jax.numpy as jnp
from jax import lax
from jax.experimental import pallas as pl
from jax.experimental.pallas import tpu as pltpu

D_MODEL = 2048
BATCH = 4
SEQ = 2048
DEPTH = 4
N_META = 16

DA_HEADS = 8
DA_HEAD_DIM = 64
DA_HEAD_WIDTH = 2 * DA_HEAD_DIM
DA_WIDTH = DA_HEADS * DA_HEAD_WIDTH
GLA_HEADS = 4
GLA_WIDTH = 1024
GLA_HEAD_V = 256
GLA_KEY_WIDTH = 512
GLA_HEAD_K = 128
GLA_GATE_RANK = 16
GLA_GATE_NORM = 16.0
GLA_CHUNK = 64
D_FF = 5632
EPS = 1e-6

COL_Q_DA = 0
COL_K_DA = DA_WIDTH
COL_V_DA = 2 * DA_WIDTH
COL_Q_G = 3 * DA_WIDTH
COL_K_G = COL_Q_G + GLA_KEY_WIDTH
COL_V_G = COL_K_G + GLA_KEY_WIDTH
COL_G_G = COL_V_G + GLA_WIDTH
COL_LR = COL_G_G + GLA_WIDTH
IN_MAIN = COL_LR
LANES = 128

VMEM_LIMIT = 56 * 1024 * 1024

F32 = jnp.float32
BF16 = jnp.bfloat16

_NT = (((1,), (1,)), ((), ()))
_TN = (((0,), (0,)), ((), ()))
_LOG2E = math.log2(math.e)


def _params(*sem):
    return pltpu.CompilerParams(dimension_semantics=sem, vmem_limit_bytes=VMEM_LIMIT)


def _rms(x, g):
    return x * lax.rsqrt(jnp.mean(x * x, axis=-1, keepdims=True) + EPS) * g


def _silu(x):
    return x * (1.0 / (1.0 + jnp.exp(-x)))


def _norm_kernel(x_ref, g_ref, o_ref):
    o_ref[...] = _rms(x_ref[...], g_ref[...]).astype(o_ref.dtype)


def _entry_norm(x, g, bm):
    m = x.shape[0]
    return pl.pallas_call(
        _norm_kernel,
        grid=(m // bm,),
        in_specs=[pl.BlockSpec((bm, D_MODEL), lambda i: (i, 0)),
                  pl.BlockSpec((1, D_MODEL), lambda i: (0, 0))],
        out_specs=pl.BlockSpec((bm, D_MODEL), lambda i: (i, 0)),
        out_shape=jax.ShapeDtypeStruct((m, D_MODEL), BF16),
        compiler_params=_params("parallel"),
        name="entry_norm",
    )(x, g)


def _mm_kernel(x_ref, w_ref, o_ref):
    o_ref[...] = jnp.dot(x_ref[...], w_ref[...], preferred_element_type=F32).astype(o_ref.dtype)


def _in_proj(hn, w_main, layer, bm, bn):
    m = hn.shape[0]
    return pl.pallas_call(
        _mm_kernel,
        grid=(m // bm, IN_MAIN // bn),
        in_specs=[pl.BlockSpec((bm, D_MODEL), lambda i, j: (i, 0)),
                  pl.BlockSpec((None, D_MODEL, bn), lambda i, j: (layer, 0, j))],
        out_specs=pl.BlockSpec((bm, bn), lambda i, j: (i, j)),
        out_shape=jax.ShapeDtypeStruct((m, IN_MAIN), BF16),
        compiler_params=_params("parallel", "arbitrary"),
        name="in_proj",
    )(hn, w_main)


def _gate_kernel(x_ref, wlr_ref, w2_ref, b2_ref, o_ref):
    lr = jnp.dot(x_ref[...], wlr_ref[...], preferred_element_type=F32)
    z = jnp.dot(lr.astype(BF16), w2_ref[...], preferred_element_type=F32) + b2_ref[...]
    log_sig = jnp.minimum(z, 0.0) - jnp.log1p(jnp.exp(-jnp.abs(z)))
    o_ref[...] = log_sig * (1.0 / GLA_GATE_NORM)


def _gate(hn, w_lr, w2, b2, layer, bm):
    m = hn.shape[0]
    return pl.pallas_call(
        _gate_kernel,
        grid=(m // bm,),
        in_specs=[pl.BlockSpec((bm, D_MODEL), lambda i: (i, 0)),
                  pl.BlockSpec((None, D_MODEL, LANES), lambda i: (layer, 0, 0)),
                  pl.BlockSpec((None, LANES, GLA_KEY_WIDTH), lambda i: (layer, 0, 0)),
                  pl.BlockSpec((None, 1, GLA_KEY_WIDTH), lambda i: (layer, 0, 0))],
        out_specs=pl.BlockSpec((bm, GLA_KEY_WIDTH), lambda i: (i, 0)),
        out_shape=jax.ShapeDtypeStruct((m, GLA_KEY_WIDTH), F32),
        compiler_params=_params("parallel"),
        name="gla_gate",
    )(hn, w_lr, w2, b2)


def _da_kernel(lam_ref, g_ref, q_ref, k_ref, v_ref, kp_ref, vp_ref, o_ref, vx_scr, vpx_scr,
               *, tq, lam_init, has_prefix):
    qi = pl.program_id(2)
    seq = v_ref.shape[0]
    hw = DA_HEAD_WIDTH

    @pl.when(qi == 0)
    def _():
        vx_scr[:, :hw] = v_ref[...]
        vx_scr[:, hw:] = jnp.ones((seq, hw), BF16)
        if has_prefix:
            vpx_scr[:, :hw] = vp_ref[...]
            vpx_scr[:, hw:] = jnp.ones((N_META, hw), BF16)

    lane = lax.broadcasted_iota(jnp.int32, (1, hw), 1)
    q = q_ref[...].astype(F32) * (DA_HEAD_DIM ** -0.5 * _LOG2E)
    qs = (jnp.where(lane < DA_HEAD_DIM, q, 0.0).astype(BF16),
          jnp.where(lane >= DA_HEAD_DIM, q, 0.0).astype(BF16))

    causal = (lax.broadcasted_iota(jnp.int32, (tq, tq), 0) >= lax.broadcasted_iota(jnp.int32, (tq, tq), 1))
    lamv = lam_ref[...]
    lam = (jnp.exp(jnp.sum(lamv[0:1] * lamv[1:2], axis=-1, keepdims=True))
           - jnp.exp(jnp.sum(lamv[2:3] * lamv[3:4], axis=-1, keepdims=True)) + lam_init)

    def attend(n_below):
        kv_len = (n_below + 1) * tq
        heads = []
        for c in range(2):
            s = lax.dot_general(qs[c], k_ref[:kv_len, :], _NT, preferred_element_type=F32)
            parts = [jnp.where(causal, s[:, n_below * tq:], -jnp.inf)]
            if n_below:
                parts.insert(0, s[:, :n_below * tq])
            m = functools.reduce(jnp.maximum, [jnp.max(part, axis=-1, keepdims=True) for part in parts])
            if has_prefix:
                sp = lax.dot_general(qs[c], kp_ref[...], _NT, preferred_element_type=F32)
                m = jnp.maximum(m, jnp.max(sp, axis=-1, keepdims=True))
            p = [jnp.exp2(part - m).astype(BF16) for part in parts]
            p = p[0] if len(p) == 1 else jnp.concatenate(p, axis=1)
            acc = jnp.dot(p, vx_scr[:kv_len, :], preferred_element_type=F32)
            if has_prefix:
                acc = acc + jnp.dot(jnp.exp2(sp - m).astype(BF16), vpx_scr[...], preferred_element_type=F32)
            heads.append(acc[:, :hw] / acc[:, hw:])
        o = heads[0] - lam * heads[1]
        o_ref[...] = (_rms(o, g_ref[...]) * (1.0 - lam_init)).astype(o_ref.dtype)

    for n in range(seq // tq):
        pl.when(qi == n)(functools.partial(attend, n))


def _diff_attention(p, p_prefix, lam, subln_g, layer, lam_init, nbatch, seq, tq, has_prefix):
    nq = seq // tq
    kcol = COL_K_DA // DA_HEAD_WIDTH
    vcol = COL_V_DA // DA_HEAD_WIDTH
    kernel = functools.partial(_da_kernel, tq=tq, lam_init=lam_init, has_prefix=has_prefix)
    return pl.pallas_call(
        kernel,
        grid=(nbatch, DA_HEADS, nq),
        in_specs=[pl.BlockSpec((None, 4, DA_HEAD_DIM), lambda b, h, i: (layer, 0, 0)),
                  pl.BlockSpec((None, 1, DA_HEAD_WIDTH), lambda b, h, i: (layer, 0, 0)),
                  pl.BlockSpec((tq, DA_HEAD_WIDTH), lambda b, h, i: (b * nq + i, h)),
                  pl.BlockSpec((seq, DA_HEAD_WIDTH), lambda b, h, i: (b, kcol + h)),
                  pl.BlockSpec((seq, DA_HEAD_WIDTH), lambda b, h, i: (b, vcol + h)),
                  pl.BlockSpec((N_META, DA_HEAD_WIDTH), lambda b, h, i: (0, kcol + h)),
                  pl.BlockSpec((N_META, DA_HEAD_WIDTH), lambda b, h, i: (0, vcol + h))],
        out_specs=pl.BlockSpec((tq, DA_HEAD_WIDTH), lambda b, h, i: (b * nq + i, h)),
        out_shape=jax.ShapeDtypeStruct((nbatch * seq, DA_WIDTH), BF16),
        scratch_shapes=[pltpu.VMEM((seq, 2 * DA_HEAD_WIDTH), BF16),
                        pltpu.VMEM((N_META, 2 * DA_HEAD_WIDTH), BF16)],
        compiler_params=_params("parallel", "parallel", "arbitrary"),
        name="diff_attention",
    )(lam, subln_g, p, p, p, p_prefix, p_prefix)


def _split_bf16(x):
    hi = x.astype(BF16)
    lo = (x - hi.astype(F32)).astype(BF16)
    return hi, lo


def _gla_kernel(q_ref, k_ref, v_ref, go_ref, gk_ref, ng_ref, s0_ref, o_ref, sfin_ref, s_scr,
                *, chunk, nchunks):
    blk = pl.program_id(2)

    @pl.when(blk == 0)
    def _():
        s_scr[...] = s0_ref[...]

    row = lax.broadcasted_iota(jnp.int32, (chunk, chunk), 0)
    col = lax.broadcasted_iota(jnp.int32, (chunk, chunk), 1)
    tril = row >= col
    tril_bf = tril.astype(F32).astype(BF16)
    ones = jnp.ones((chunk, GLA_HEAD_V), BF16)
    ng = ng_ref[...]

    for c in range(nchunks):
        sl = pl.ds(c * chunk, chunk)
        gk_hi, gk_lo = _split_bf16(gk_ref[sl, :])
        b = (jnp.dot(tril_bf, gk_hi, preferred_element_type=F32)
             + jnp.dot(tril_bf, gk_lo, preferred_element_type=F32))
        b_last_col = (lax.dot_general(gk_hi, ones, _TN, preferred_element_type=F32)
                      + lax.dot_general(gk_lo, ones, _TN, preferred_element_type=F32))
        b_last = b[chunk - 1:chunk, :]
        q = q_ref[sl, :].astype(F32) * (GLA_HEAD_K ** -0.5)
        k = k_ref[sl, :].astype(F32)
        v = v_ref[sl, :]
        q_in = (q * jnp.exp(b)).astype(BF16)
        k_in = (k * jnp.exp(-b)).astype(BF16)
        k_dec = (k * jnp.exp(b_last - b)).astype(BF16)
        a = lax.dot_general(q_in, k_in, _NT, preferred_element_type=F32)
        a = jnp.where(tril, a, 0.0)
        s_prev = s_scr[...]
        o = (jnp.dot(a.astype(BF16), v, preferred_element_type=F32)
             + jnp.dot(q_in, s_prev.astype(BF16), preferred_element_type=F32))
        s_scr[...] = jnp.exp(b_last_col) * s_prev + lax.dot_general(k_dec, v, _TN, preferred_element_type=F32)
        go = go_ref[sl, :].astype(F32)
        o_ref[sl, :] = (_rms(o, ng) * _silu(go)).astype(o_ref.dtype)

    @pl.when(blk == pl.num_programs(2) - 1)
    def _():
        sfin_ref[...] = s_scr[...]


def _gla(p, gk, norm_g, s0, layer, nbatch, seq, chunk, nchunks):
    blk_rows = chunk * nchunks
    nblk = seq // blk_rows
    qcol = COL_Q_G // GLA_HEAD_K
    kcol = COL_K_G // GLA_HEAD_K
    vcol = COL_V_G // GLA_HEAD_V
    gcol = COL_G_G // GLA_HEAD_V
    kernel = functools.partial(_gla_kernel, chunk=chunk, nchunks=nchunks)
    return pl.pallas_call(
        kernel,
        grid=(nbatch, GLA_HEADS, nblk),
        in_specs=[pl.BlockSpec((blk_rows, GLA_HEAD_K), lambda b, h, i: (b * nblk + i, qcol + h)),
                  pl.BlockSpec((blk_rows, GLA_HEAD_K), lambda b, h, i: (b * nblk + i, kcol + h)),
                  pl.BlockSpec((blk_rows, GLA_HEAD_V), lambda b, h, i: (b * nblk + i, vcol + h)),
                  pl.BlockSpec((blk_rows, GLA_HEAD_V), lambda b, h, i: (b * nblk + i, gcol + h)),
                  pl.BlockSpec((blk_rows, GLA_HEAD_K), lambda b, h, i: (b * nblk + i, h)),
                  pl.BlockSpec((None, 1, GLA_HEAD_V), lambda b, h, i: (layer, 0, 0)),
                  pl.BlockSpec((None, GLA_HEAD_K, GLA_HEAD_V), lambda b, h, i: (h, 0, 0))],
        out_specs=[pl.BlockSpec((blk_rows, GLA_HEAD_V), lambda b, h, i: (b * nblk + i, h)),
                   pl.BlockSpec((None, GLA_HEAD_K, GLA_HEAD_V), lambda b, h, i: (b * GLA_HEADS + h, 0, 0))],
        out_shape=[jax.ShapeDtypeStruct((nbatch * seq, GLA_WIDTH), BF16),
                   jax.ShapeDtypeStruct((nbatch * GLA_HEADS, GLA_HEAD_K, GLA_HEAD_V), F32)],
        scratch_shapes=[pltpu.VMEM((GLA_HEAD_K, GLA_HEAD_V), F32)],
        compiler_params=_params("parallel", "parallel", "arbitrary"),
        name="gla",
    )(p, p, p, p, gk, norm_g, s0)


def _residual_epilogue(y, h_ref, gpost_ref, gnext_ref, ho_ref, hn_ref):
    h = h_ref[...] + _rms(y, gpost_ref[...])
    ho_ref[...] = h
    hn_ref[...] = _rms(h, gnext_ref[...]).astype(hn_ref.dtype)


def _out_proj_kernel(xa_ref, xb_ref, w_ref, h_ref, gpost_ref, gnext_ref, ho_ref, hn_ref):
    y = (jnp.dot(xa_ref[...], w_ref[:DA_WIDTH, :], preferred_element_type=F32)
         + jnp.dot(xb_ref[...], w_ref[DA_WIDTH:, :], preferred_element_type=F32))
    _residual_epilogue(y, h_ref, gpost_ref, gnext_ref, ho_ref, hn_ref)


def _out_proj(o_da, o_gla, w_out, h, g_post, g_next, layer, bm):
    m = h.shape[0]
    row = lambda i: (i, 0)
    return pl.pallas_call(
        _out_proj_kernel,
        grid=(m // bm,),
        in_specs=[pl.BlockSpec((bm, DA_WIDTH), row),
                  pl.BlockSpec((bm, GLA_WIDTH), row),
                  pl.BlockSpec((None, D_MODEL, D_MODEL), lambda i: (layer, 0, 0)),
                  pl.BlockSpec((bm, D_MODEL), row),
                  pl.BlockSpec((1, D_MODEL), lambda i: (0, 0)),
                  pl.BlockSpec((1, D_MODEL), lambda i: (0, 0))],
        out_specs=[pl.BlockSpec((bm, D_MODEL), row), pl.BlockSpec((bm, D_MODEL), row)],
        out_shape=[jax.ShapeDtypeStruct((m, D_MODEL), F32), jax.ShapeDtypeStruct((m, D_MODEL), BF16)],
        compiler_params=_params("parallel"),
        name="out_proj",
    )(o_da, o_gla, w_out, h, g_post, g_next)


def _down_proj_kernel(x_ref, w_ref, h_ref, gpost_ref, gnext_ref, ho_ref, hn_ref, acc_scr):
    kk = pl.program_id(1)

    @pl.when(kk == 0)
    def _():
        acc_scr[...] = jnp.zeros_like(acc_scr)

    acc_scr[...] += jnp.dot(x_ref[...], w_ref[...], preferred_element_type=F32)

    @pl.when(kk == pl.num_programs(1) - 1)
    def _():
        _residual_epilogue(acc_scr[...], h_ref, gpost_ref, gnext_ref, ho_ref, hn_ref)


def _down_proj(x, w_down, h, g_post, g_next, layer, bm, bk):
    m = h.shape[0]
    row = lambda i, k: (i, 0)
    return pl.pallas_call(
        _down_proj_kernel,
        grid=(m // bm, D_FF // bk),
        in_specs=[pl.BlockSpec((bm, bk), lambda i, k: (i, k)),
                  pl.BlockSpec((None, bk, D_MODEL), lambda i, k: (layer, k, 0)),
                  pl.BlockSpec((bm, D_MODEL), row),
                  pl.BlockSpec((1, D_MODEL), lambda i, k: (0, 0)),
                  pl.BlockSpec((1, D_MODEL), lambda i, k: (0, 0))],
        out_specs=[pl.BlockSpec((bm, D_MODEL), row), pl.BlockSpec((bm, D_MODEL), row)],
        out_shape=[jax.ShapeDtypeStruct((m, D_MODEL), F32), jax.ShapeDtypeStruct((m, D_MODEL), BF16)],
        scratch_shapes=[pltpu.VMEM((bm, D_MODEL), F32)],
        compiler_params=_params("parallel", "arbitrary"),
        name="down_proj",
    )(x, w_down, h, g_post, g_next)


_HALO = 8
_UP_SUB_ROWS = 256


def _up_conv_kernel(x_ref, wa_ref, wv_ref, cwa_ref, cwv_ref, cba_ref, cbv_ref, ha_ref, hv_ref, *rest,
                    bm, sub_rows, blocks_per_seq, emit_u):
    if emit_u:
        o_ref, ua_ref, uv_ref, ubuf = rest
    else:
        o_ref, ubuf = rest
    i = pl.program_id(1)
    first = (i % blocks_per_seq) == 0
    halves = ((wa_ref, cwa_ref, cba_ref, ha_ref), (wv_ref, cwv_ref, cbv_ref, hv_ref))
    for idx, (_, _, _, h_ref) in enumerate(halves):
        buf = ubuf.at[idx]

        @pl.when(first)
        def _():
            buf[_HALO - 2:_HALO, :] = h_ref[N_META - 2:N_META, :]

        @pl.when(jnp.logical_not(first))
        def _():
            buf[_HALO - 2:_HALO, :] = buf[_HALO + bm - 2:_HALO + bm, :]

    sb = min(bm, sub_rows)
    nsub = bm // sb

    def matmuls(r):
        x = x_ref[r * sb:(r + 1) * sb, :]
        for idx, (w_ref, _, _, _) in enumerate(halves):
            u = jnp.dot(x, w_ref[...], preferred_element_type=F32)
            ubuf[idx, _HALO + r * sb:_HALO + (r + 1) * sb, :] = u
            if emit_u:
                (ua_ref, uv_ref)[idx][r * sb:(r + 1) * sb, :] = u

    def conv_gate(r):
        lo = _HALO + r * sb
        conv = []
        for idx, (_, cw_ref, cb_ref, _) in enumerate(halves):
            cw = cw_ref[...]
            conv.append(cb_ref[...]
                        + cw[0:1] * ubuf[idx, lo - 2:lo - 2 + sb, :]
                        + cw[1:2] * ubuf[idx, lo - 1:lo - 1 + sb, :]
                        + cw[2:3] * ubuf[idx, lo:lo + sb, :])
        o_ref[r * sb:(r + 1) * sb, :] = (_silu(conv[0]) * conv[1]).astype(o_ref.dtype)

    matmuls(0)
    for r in range(1, nsub):
        matmuls(r)
        conv_gate(r - 1)
    conv_gate(nsub - 1)


def _up_conv(hn, w_up, conv_w, conv_b, hist_a, hist_v, layer, bm, bn, blocks_per_seq, emit_u):
    m = hn.shape[0]
    nj = D_FF // bn
    kernel = functools.partial(_up_conv_kernel, bm=bm, sub_rows=_UP_SUB_ROWS, blocks_per_seq=blocks_per_seq,
                               emit_u=emit_u)
    out_specs = [pl.BlockSpec((bm, bn), lambda j, i: (i, j))]
    out_shape = [jax.ShapeDtypeStruct((m, D_FF), BF16)]
    if emit_u:
        out_specs += [pl.BlockSpec((bm, bn), lambda j, i: (i, j))] * 2
        out_shape += [jax.ShapeDtypeStruct((m, D_FF), F32)] * 2
    return pl.pallas_call(
        kernel,
        grid=(nj, m // bm),
        in_specs=[pl.BlockSpec((bm, D_MODEL), lambda j, i: (i, 0)),
                  pl.BlockSpec((None, D_MODEL, bn), lambda j, i: (layer, 0, j)),
                  pl.BlockSpec((None, D_MODEL, bn), lambda j, i: (layer, 0, j + nj)),
                  pl.BlockSpec((None, 3, bn), lambda j, i: (layer, 0, j)),
                  pl.BlockSpec((None, 3, bn), lambda j, i: (layer, 0, j + nj)),
                  pl.BlockSpec((None, 1, bn), lambda j, i: (layer, 0, j)),
                  pl.BlockSpec((None, 1, bn), lambda j, i: (layer, 0, j + nj)),
                  pl.BlockSpec((N_META, bn), lambda j, i: (0, j)),
                  pl.BlockSpec((N_META, bn), lambda j, i: (0, j))],
        out_specs=out_specs,
        out_shape=out_shape,
        scratch_shapes=[pltpu.VMEM((2, _HALO + bm, bn), F32)],
        compiler_params=_params("parallel", "arbitrary"),
        name="up_conv",
    )(hn, w_up, w_up, conv_w, conv_w, conv_b, conv_b, hist_a, hist_v)


def _lam_init(layer):
    return 0.8 - 0.6 * math.exp(-0.3 * layer)


def _trunk(h, prm, side_in, *, nbatch, seq, cfg):
    is_meta = side_in is None
    hn = _entry_norm(h, prm["pre_mix_g"][0], cfg["bm_norm"])
    side_out = []
    for l in range(DEPTH):
        lam_init = _lam_init(l)
        p = _in_proj(hn, prm["w_in"], l, cfg["bm_in"], cfg["bn_in"])
        gk = _gate(hn, prm["w_lr"], prm["w2"], prm["b2"], l, cfg["bm_gate"])
        if is_meta:
            p_prefix = p
            s0 = jnp.zeros((GLA_HEADS, GLA_HEAD_K, GLA_HEAD_V), F32)
            hist_a = hist_v = jnp.zeros((N_META, D_FF), F32)
        else:
            p_prefix, s0, hist_a, hist_v = side_in[l]
        o_da = _diff_attention(p, p_prefix, prm["da_lambda"], prm["da_subln_g"], l, lam_init,
                               nbatch, seq, cfg["tq"], has_prefix=not is_meta)
        o_gla, s_fin = _gla(p, gk, prm["gla_norm_g"], s0, l, nbatch, seq, cfg["chunk"], cfg["nchunks"])
        h, hn = _out_proj(o_da, o_gla, prm["w_out"], h, prm["post_mix_g"][l], prm["pre_ffn_g"][l], l, cfg["bm_out"])
        ffn = _up_conv(hn, prm["w_up"], prm["conv_w"], prm["conv_b"], hist_a, hist_v, l,
                       cfg["bm_up"], cfg["bn_up"], seq // cfg["bm_up"], emit_u=is_meta)
        if is_meta:
            act, u_a, u_v = ffn
            side_out.append((p, s_fin, u_a, u_v))
        else:
            (act,) = ffn
        g_next = prm["pre_mix_g"][(l + 1) % DEPTH]
        h, hn = _down_proj(act, prm["w_down"], h, prm["post_ffn_g"][l], g_next, l, cfg["bm_down"], cfg["bk_down"])
    return h, side_out


_META_CFG = dict(bm_norm=N_META, bm_in=N_META, bn_in=1024, bm_gate=N_META, tq=N_META, chunk=N_META, nchunks=1,
                 bm_out=N_META, bm_up=N_META, bn_up=512, bm_down=N_META, bk_down=1408)
_SEQ_CFG = dict(bm_norm=512, bm_in=1024, bn_in=512, bm_gate=1024, tq=512, chunk=GLA_CHUNK, nchunks=32,
                bm_out=512, bm_up=1024, bn_up=512, bm_down=512, bk_down=1408)


def kernel(x, meta_tokens, pre_mix_g, w_in, da_lambda, da_subln_g, gla_gate_w2, gla_gate_b, gla_norm_g, w_out,
           post_mix_g, pre_ffn_g, w_up, conv_w, conv_b, w_down, post_ffn_g):
    vec = lambda a: a.astype(F32).reshape(DEPTH, 1, a.shape[-1])
    prm = dict(
        pre_mix_g=vec(pre_mix_g), post_mix_g=vec(post_mix_g), pre_ffn_g=vec(pre_ffn_g), post_ffn_g=vec(post_ffn_g),
        w_in=w_in[:, :, :IN_MAIN].astype(BF16),
        w_lr=jnp.pad(w_in[:, :, IN_MAIN:], ((0, 0), (0, 0), (0, LANES - GLA_GATE_RANK))).astype(BF16),
        w2=jnp.pad(gla_gate_w2, ((0, 0), (0, LANES - GLA_GATE_RANK), (0, 0))).astype(BF16),
        b2=vec(gla_gate_b),
        da_lambda=da_lambda.astype(F32), da_subln_g=vec(da_subln_g), gla_norm_g=vec(gla_norm_g),
        w_out=w_out.astype(BF16), w_up=w_up.astype(BF16), w_down=w_down.astype(BF16),
        conv_w=conv_w.astype(F32), conv_b=vec(conv_b),
    )
    _, side = _trunk(meta_tokens.astype(F32), prm, None, nbatch=1, seq=N_META, cfg=_META_CFG)
    h, _ = _trunk(x.astype(F32).reshape(BATCH * SEQ, D_MODEL), prm, side, nbatch=BATCH, seq=SEQ, cfg=_SEQ_CFG)
    return h.reshape(BATCH, SEQ, D_MODEL).astype(x.dtype)
```

```python
import functools
import math

import jax
import jax.numpy as jnp
from jax import lax
from jax.experimental import pallas as pl
from jax.experimental.pallas import tpu as pltpu

D_MODEL = 2048
BATCH = 4
SEQ = 2048
DEPTH = 4
N_META = 16

DA_HEADS = 8
DA_HEAD_DIM = 64
DA_HEAD_WIDTH = 2 * DA_HEAD_DIM
DA_WIDTH = DA_HEADS * DA_HEAD_WIDTH
GLA_HEADS = 4
GLA_WIDTH = 1024
GLA_HEAD_V = 256
GLA_KEY_WIDTH = 512
GLA_HEAD_K = 128
GLA_GATE_RANK = 16
GLA_GATE_NORM = 16.0
GLA_CHUNK = 64
D_FF = 5632
EPS = 1e-6

COL_Q_DA = 0
COL_K_DA = DA_WIDTH
COL_V_DA = 2 * DA_WIDTH
COL_Q_G = 3 * DA_WIDTH
COL_K_G = COL_Q_G + GLA_KEY_WIDTH
COL_V_G = COL_K_G + GLA_KEY_WIDTH
COL_G_G = COL_V_G + GLA_WIDTH
COL_LR = COL_G_G + GLA_WIDTH
IN_MAIN = COL_LR
LANES = 128

VMEM_LIMIT = 56 * 1024 * 1024

F32 = jnp.float32
BF16 = jnp.bfloat16

_NT = (((1,), (1,)), ((), ()))
_TN = (((0,), (0,)), ((), ()))
_LOG2E = math.log2(math.e)


def _params(*sem):
    return pltpu.CompilerParams(dimension_semantics=sem, vmem_limit_bytes=VMEM_LIMIT)


def _rms(x, g):
    return x * lax.rsqrt(jnp.mean(x * x, axis=-1, keepdims=True) + EPS) * g


def _silu(x):
    return x * (1.0 / (1.0 + jnp.exp(-x)))


def _norm_kernel(x_ref, g_ref, o_ref):
    o_ref[...] = _rms(x_ref[...], g_ref[...]).astype(o_ref.dtype)


def _entry_norm(x, g, bm):
    m = x.shape[0]
    return pl.pallas_call(
        _norm_kernel,
        grid=(m // bm,),
        in_specs=[pl.BlockSpec((bm, D_MODEL), lambda i: (i, 0)),
                  pl.BlockSpec((1, D_MODEL), lambda i: (0, 0))],
        out_specs=pl.BlockSpec((bm, D_MODEL), lambda i: (i, 0)),
        out_shape=jax.ShapeDtypeStruct((m, D_MODEL), BF16),
        compiler_params=_params("parallel"),
        name="entry_norm",
    )(x, g)


def _mm_kernel(x_ref, w_ref, o_ref):
    o_ref[...] = jnp.dot(x_ref[...], w_ref[...], preferred_element_type=F32).astype(o_ref.dtype)


def _in_proj(hn, w_main, layer, bm, bn):
    m = hn.shape[0]
    return pl.pallas_call(
        _mm_kernel,
        grid=(m // bm, IN_MAIN // bn),
        in_specs=[pl.BlockSpec((bm, D_MODEL), lambda i, j: (i, 0)),
                  pl.BlockSpec((None, D_MODEL, bn), lambda i, j: (layer, 0, j))],
        out_specs=pl.BlockSpec((bm, bn), lambda i, j: (i, j)),
        out_shape=jax.ShapeDtypeStruct((m, IN_MAIN), BF16),
        compiler_params=_params("parallel", "arbitrary"),
        name="in_proj",
    )(hn, w_main)


def _gate_kernel(x_ref, wlr_ref, w2_ref, b2_ref, o_ref):
    lr = jnp.dot(x_ref[...], wlr_ref[...], preferred_element_type=F32)
    z = jnp.dot(lr.astype(BF16), w2_ref[...], preferred_element_type=F32) + b2_ref[...]
    log_sig = jnp.minimum(z, 0.0) - jnp.log1p(jnp.exp(-jnp.abs(z)))
    o_ref[...] = log_sig * (1.0 / GLA_GATE_NORM)


def _gate(hn, w_lr, w2, b2, layer, bm):
    m = hn.shape[0]
    return pl.pallas_call(
        _gate_kernel,
        grid=(m // bm,),
        in_specs=[pl.BlockSpec((bm, D_MODEL), lambda i: (i, 0)),
                  pl.BlockSpec((None, D_MODEL, LANES), lambda i: (layer, 0, 0)),
                  pl.BlockSpec((None, LANES, GLA_KEY_WIDTH), lambda i: (layer, 0, 0)),
                  pl.BlockSpec((None, 1, GLA_KEY_WIDTH), lambda i: (layer, 0, 0))],
        out_specs=pl.BlockSpec((bm, GLA_KEY_WIDTH), lambda i: (i, 0)),
        out_shape=jax.ShapeDtypeStruct((m, GLA_KEY_WIDTH), F32),
        compiler_params=_params("parallel"),
        name="gla_gate",
    )(hn, w_lr, w2, b2)


def _da_kernel(lam_ref, g_ref, q_ref, k_ref, v_ref, kp_ref, vp_ref, o_ref, vx_scr, vpx_scr,
               *, tq, lam_init, has_prefix):
    qi = pl.program_id(2)
    seq = v_ref.shape[0]
    hw = DA_HEAD_WIDTH

    @pl.when(qi == 0)
    def _():
        vx_scr[:, :hw] = v_ref[...]
        vx_scr[:, hw:] = jnp.ones((seq, hw), BF16)
        if has_prefix:
            vpx_scr[:, :hw] = vp_ref[...]
            vpx_scr[:, hw:] = jnp.ones((N_META, hw), BF16)

    lane = lax.broadcasted_iota(jnp.int32, (1, hw), 1)
    q = q_ref[...].astype(F32) * (DA_HEAD_DIM ** -0.5 * _LOG2E)
    qs = (jnp.where(lane < DA_HEAD_DIM, q, 0.0).astype(BF16),
          jnp.where(lane >= DA_HEAD_DIM, q, 0.0).astype(BF16))

    causal = (lax.broadcasted_iota(jnp.int32, (tq, tq), 0) >= lax.broadcasted_iota(jnp.int32, (tq, tq), 1))
    lamv = lam_ref[...]
    lam = (jnp.exp(jnp.sum(lamv[0:1] * lamv[1:2], axis=-1, keepdims=True))
           - jnp.exp(jnp.sum(lamv[2:3] * lamv[3:4], axis=-1, keepdims=True)) + lam_init)

    def attend(n_below):
        kv_len = (n_below + 1) * tq
        heads = []
        for c in range(2):
            s = lax.dot_general(qs[c], k_ref[:kv_len, :], _NT, preferred_element_type=F32)
            parts = [jnp.where(causal, s[:, n_below * tq:], -jnp.inf)]
            if n_below:
                parts.insert(0, s[:, :n_below * tq])
            m = functools.reduce(jnp.maximum, [jnp.max(part, axis=-1, keepdims=True) for part in parts])
            if has_prefix:
                sp = lax.dot_general(qs[c], kp_ref[...], _NT, preferred_element_type=F32)
                m = jnp.maximum(m, jnp.max(sp, axis=-1, keepdims=True))
            p = [jnp.exp2(part - m).astype(BF16) for part in parts]
            p = p[0] if len(p) == 1 else jnp.concatenate(p, axis=1)
            acc = jnp.dot(p, vx_scr[:kv_len, :], preferred_element_type=F32)
            if has_prefix:
                acc = acc + jnp.dot(jnp.exp2(sp - m).astype(BF16), vpx_scr[...], preferred_element_type=F32)
            heads.append(acc[:, :hw] / acc[:, hw:])
        o = heads[0] - lam * heads[1]
        o_ref[...] = (_rms(o, g_ref[...]) * (1.0 - lam_init)).astype(o_ref.dtype)

    for n in range(seq // tq):
        pl.when(qi == n)(functools.partial(attend, n))


def _diff_attention(p, p_prefix, lam, subln_g, layer, lam_init, nbatch, seq, tq, has_prefix):
    nq = seq // tq
    kcol = COL_K_DA // DA_HEAD_WIDTH
    vcol = COL_V_DA // DA_HEAD_WIDTH
    kernel = functools.partial(_da_kernel, tq=tq, lam_init=lam_init, has_prefix=has_prefix)
    return pl.pallas_call(
        kernel,
        grid=(nbatch, DA_HEADS, nq),
        in_specs=[pl.BlockSpec((None, 4, DA_HEAD_DIM), lambda b, h, i: (layer, 0, 0)),
                  pl.BlockSpec((None, 1, DA_HEAD_WIDTH), lambda b, h, i: (layer, 0, 0)),
                  pl.BlockSpec((tq, DA_HEAD_WIDTH), lambda b, h, i: (b * nq + i, h)),
                  pl.BlockSpec((seq, DA_HEAD_WIDTH), lambda b, h, i: (b, kcol + h)),
                  pl.BlockSpec((seq, DA_HEAD_WIDTH), lambda b, h, i: (b, vcol + h)),
                  pl.BlockSpec((N_META, DA_HEAD_WIDTH), lambda b, h, i: (0, kcol + h)),
                  pl.BlockSpec((N_META, DA_HEAD_WIDTH), lambda b, h, i: (0, vcol + h))],
        out_specs=pl.BlockSpec((tq, DA_HEAD_WIDTH), lambda b, h, i: (b * nq + i, h)),
        out_shape=jax.ShapeDtypeStruct((nbatch * seq, DA_WIDTH), BF16),
        scratch_shapes=[pltpu.VMEM((seq, 2 * DA_HEAD_WIDTH), BF16),
                        pltpu.VMEM((N_META, 2 * DA_HEAD_WIDTH), BF16)],
        compiler_params=_params("parallel", "parallel", "arbitrary"),
        name="diff_attention",
    )(lam, subln_g, p, p, p, p_prefix, p_prefix)


def _split_bf16(x):
    hi = x.astype(BF16)
    lo = (x - hi.astype(F32)).astype(BF16)
    return hi, lo


def _gla_kernel(q_ref, k_ref, v_ref, go_ref, gk_ref, ng_ref, s0_ref, o_ref, sfin_ref, s_scr,
                *, chunk, nchunks):
    blk = pl.program_id(2)

    @pl.when(blk == 0)
    def _():
        s_scr[...] = s0_ref[...]

    row = lax.broadcasted_iota(jnp.int32, (chunk, chunk), 0)
    col = lax.broadcasted_iota(jnp.int32, (chunk, chunk), 1)
    tril = row >= col
    tril_bf = tril.astype(F32).astype(BF16)
    ones = jnp.ones((chunk, GLA_HEAD_V), BF16)
    ng = ng_ref[...]

    for c in range(nchunks):
        sl = pl.ds(c * chunk, chunk)
        gk_hi, gk_lo = _split_bf16(gk_ref[sl, :])
        b = (jnp.dot(tril_bf, gk_hi, preferred_element_type=F32)
             + jnp.dot(tril_bf, gk_lo, preferred_element_type=F32))
        b_last_col = (lax.dot_general(gk_hi, ones, _TN, preferred_element_type=F32)
                      + lax.dot_general(gk_lo, ones, _TN, preferred_element_type=F32))
        b_last = b[chunk - 1:chunk, :]
        q = q_ref[sl, :].astype(F32) * (GLA_HEAD_K ** -0.5)
        k = k_ref[sl, :].astype(F32)
        v = v_ref[sl, :]
        q_in = (q * jnp.exp(b)).astype(BF16)
        k_in = (k * jnp.exp(-b)).astype(BF16)
        k_dec = (k * jnp.exp(b_last - b)).astype(BF16)
        a = lax.dot_general(q_in, k_in, _NT, preferred_element_type=F32)
        a = jnp.where(tril, a, 0.0)
        s_prev = s_scr[...]
        o = (jnp.dot(a.astype(BF16), v, preferred_element_type=F32)
             + jnp.dot(q_in, s_prev.astype(BF16), preferred_element_type=F32))
        s_scr[...] = jnp.exp(b_last_col) * s_prev + lax.dot_general(k_dec, v, _TN, preferred_element_type=F32)
        go = go_ref[sl, :].astype(F32)
        o_ref[sl, :] = (_rms(o, ng) * _silu(go)).astype(o_ref.dtype)

    @pl.when(blk == pl.num_programs(2) - 1)
    def _():
        sfin_ref[...] = s_scr[...]


def _gla(p, gk, norm_g, s0, layer, nbatch, seq, chunk, nchunks):
    blk_rows = chunk * nchunks
    nblk = seq // blk_rows
    qcol = COL_Q_G // GLA_HEAD_K
    kcol = COL_K_G // GLA_HEAD_K
    vcol = COL_V_G // GLA_HEAD_V
    gcol = COL_G_G // GLA_HEAD_V
    kernel = functools.partial(_gla_kernel, chunk=chunk, nchunks=nchunks)
    return pl.pallas_call(
        kernel,
        grid=(nbatch, GLA_HEADS, nblk),
        in_specs=[pl.BlockSpec((blk_rows, GLA_HEAD_K), lambda b, h, i: (b * nblk + i, qcol + h)),
                  pl.BlockSpec((blk_rows, GLA_HEAD_K), lambda b, h, i: (b * nblk + i, kcol + h)),
                  pl.BlockSpec((blk_rows, GLA_HEAD_V), lambda b, h, i: (b * nblk + i, vcol + h)),
                  pl.BlockSpec((blk_rows, GLA_HEAD_V), lambda b, h, i: (b * nblk + i, gcol + h)),
                  pl.BlockSpec((blk_rows, GLA_HEAD_K), lambda b, h, i: (b * nblk + i, h)),
                  pl.BlockSpec((None, 1, GLA_HEAD_V), lambda b, h, i: (layer, 0, 0)),
                  pl.BlockSpec((None, GLA_HEAD_K, GLA_HEAD_V), lambda b, h, i: (h, 0, 0))],
        out_specs=[pl.BlockSpec((blk_rows, GLA_HEAD_V), lambda b, h, i: (b * nblk + i, h)),
                   pl.BlockSpec((None, GLA_HEAD_K, GLA_HEAD_V), lambda b, h, i: (b * GLA_HEADS + h, 0, 0))],
        out_shape=[jax.ShapeDtypeStruct((nbatch * seq, GLA_WIDTH), BF16),
                   jax.ShapeDtypeStruct((nbatch * GLA_HEADS, GLA_HEAD_K, GLA_HEAD_V), F32)],
        scratch_shapes=[pltpu.VMEM((GLA_HEAD_K, GLA_HEAD_V), F32)],
        compiler_params=_params("parallel", "parallel", "arbitrary"),
        name="gla",
    )(p, p, p, p, gk, norm_g, s0)


def _deferred(step, nblocks, produce, finish):
    def emit(produce_parts, finish_parts):
        for i in range(max(len(produce_parts), len(finish_parts))):
            for parts in (produce_parts, finish_parts):
                if i < len(parts):
                    parts[i]()

    @pl.when(step == 0)
    def _():
        emit(produce(0), [])

    for slot in range(2):
        @pl.when(jnp.logical_and(jnp.logical_and(step > 0, step < nblocks), step % 2 == slot))
        def _(slot=slot):
            emit(produce(slot), finish(1 - slot))

    @pl.when(step == nblocks)
    def _():
        emit([], finish((nblocks - 1) % 2))


def _cur_block(s, nblocks):
    return jnp.minimum(s, nblocks - 1)


def _prev_block(s):
    return jnp.maximum(s - 1, 0)


_PROJ_PARTS = 4


def _residual_rows(rows, y_ref, h_ref, gpost_ref, gnext_ref, ho_ref, hn_ref):
    h = h_ref[rows, :] + _rms(y_ref[rows, :], gpost_ref[...])
    ho_ref[rows, :] = h
    hn_ref[rows, :] = _rms(h, gnext_ref[...]).astype(hn_ref.dtype)


def _proj_residual_kernel(*refs, nblocks, k_splits):
    nx = len(k_splits)
    x_refs, (w_ref, h_ref, gpost_ref, gnext_ref, ho_ref, hn_ref, y_scr) = refs[:nx], refs[nx:]
    bm, n = h_ref.shape
    nparts = _PROJ_PARTS if bm % (8 * _PROJ_PARTS) == 0 else 1

    def produce(slot):
        def part(c):
            cols = slice(c * (n // nparts), (c + 1) * (n // nparts))
            y, lo = None, 0
            for x_ref, k in zip(x_refs, k_splits):
                t = jnp.dot(x_ref[...], w_ref[lo:lo + k, cols], preferred_element_type=F32)
                y = t if y is None else y + t
                lo += k
            y_scr[slot, :, cols] = y
        return [functools.partial(part, c) for c in range(nparts)]

    def finish(slot):
        def part(r):
            rows = slice(r * (bm // nparts), (r + 1) * (bm // nparts))
            _residual_rows(rows, y_scr.at[slot], h_ref, gpost_ref, gnext_ref, ho_ref, hn_ref)
        return [functools.partial(part, r) for r in range(nparts)]

    _deferred(pl.program_id(0), nblocks, produce, finish)


def _proj_residual(xs, w, h, g_post, g_next, layer, bm, name):
    m = h.shape[0]
    nblocks = m // bm
    k_splits = tuple(x.shape[1] for x in xs)
    k_total = sum(k_splits)
    cur = lambda s: (_cur_block(s, nblocks), 0)
    prev = lambda s: (_prev_block(s), 0)
    const = lambda s: (0, 0)
    kernel = functools.partial(_proj_residual_kernel, nblocks=nblocks, k_splits=k_splits)
    return pl.pallas_call(
        kernel,
        grid=(nblocks + 1,),
        in_specs=[pl.BlockSpec((bm, k), cur) for k in k_splits] + [
            pl.BlockSpec((None, k_total, D_MODEL), lambda s: (layer, 0, 0), pipeline_mode=pl.Buffered(1)),
            pl.BlockSpec((bm, D_MODEL), prev),
            pl.BlockSpec((1, D_MODEL), const),
            pl.BlockSpec((1, D_MODEL), const)],
        out_specs=[pl.BlockSpec((bm, D_MODEL), prev), pl.BlockSpec((bm, D_MODEL), prev)],
        out_shape=[jax.ShapeDtypeStruct((m, D_MODEL), F32), jax.ShapeDtypeStruct((m, D_MODEL), BF16)],
        scratch_shapes=[pltpu.VMEM((2, bm, D_MODEL), F32)],
        compiler_params=_params("arbitrary"),
        name=name,
    )(*xs, w, h, g_post, g_next)


_HALO = 8
_UP_SUB_ROWS = 256


def _up_conv_kernel(x_ref, wa_ref, wv_ref, cwa_ref, cwv_ref, cba_ref, cbv_ref, ha_ref, hv_ref, *rest,
                    bm, sub_rows, blocks_per_seq, emit_u):
    if emit_u:
        o_ref, ua_ref, uv_ref, ubuf = rest
    else:
        o_ref, ubuf = rest
    i = pl.program_id(1)
    first = (i % blocks_per_seq) == 0
    halves = ((wa_ref, cwa_ref, cba_ref, ha_ref), (wv_ref, cwv_ref, cbv_ref, hv_ref))
    for idx, (_, _, _, h_ref) in enumerate(halves):
        buf = ubuf.at[idx]

        @pl.when(first)
        def _():
            buf[_HALO - 2:_HALO, :] = h_ref[N_META - 2:N_META, :]

        @pl.when(jnp.logical_not(first))
        def _():
            buf[_HALO - 2:_HALO, :] = buf[_HALO + bm - 2:_HALO + bm, :]

    sb = min(bm, sub_rows)
    for r in range(bm // sb):
        rows = slice(r * sb, (r + 1) * sb)
        lo = _HALO + r * sb
        x = x_ref[rows, :]
        conv = []
        for idx, (w_ref, cw_ref, cb_ref, _) in enumerate(halves):
            u = jnp.dot(x, w_ref[...], preferred_element_type=F32)
            buf = ubuf.at[idx]
            buf[lo:lo + sb, :] = u
            cw = cw_ref[...]
            conv.append(cb_ref[...]
                        + cw[0:1] * buf[lo - 2:lo - 2 + sb, :]
                        + cw[1:2] * buf[lo - 1:lo - 1 + sb, :]
                        + cw[2:3] * u)
            if emit_u:
                (ua_ref, uv_ref)[idx][rows, :] = u
        o_ref[rows, :] = (_silu(conv[0]) * conv[1]).astype(o_ref.dtype)


def _up_conv(hn, w_up, conv_w, conv_b, hist_a, hist_v, layer, bm, bn, blocks_per_seq, emit_u):
    m = hn.shape[0]
    nj = D_FF // bn
    kernel = functools.partial(_up_conv_kernel, bm=bm, sub_rows=_UP_SUB_ROWS, blocks_per_seq=blocks_per_seq,
                               emit_u=emit_u)
    out_specs = [pl.BlockSpec((bm, bn), lambda j, i: (i, j))]
    out_shape = [jax.ShapeDtypeStruct((m, D_FF), BF16)]
    if emit_u:
        out_specs += [pl.BlockSpec((bm, bn), lambda j, i: (i, j))] * 2
        out_shape += [jax.ShapeDtypeStruct((m, D_FF), F32)] * 2
    return pl.pallas_call(
        kernel,
        grid=(nj, m // bm),
        in_specs=[pl.BlockSpec((bm, D_MODEL), lambda j, i: (i, 0)),
                  pl.BlockSpec((None, D_MODEL, bn), lambda j, i: (layer, 0, j)),
                  pl.BlockSpec((None, D_MODEL, bn), lambda j, i: (layer, 0, j + nj)),
                  pl.BlockSpec((None, 3, bn), lambda j, i: (layer, 0, j)),
                  pl.BlockSpec((None, 3, bn), lambda j, i: (layer, 0, j + nj)),
                  pl.BlockSpec((None, 1, bn), lambda j, i: (layer, 0, j)),
                  pl.BlockSpec((None, 1, bn), lambda j, i: (layer, 0, j + nj)),
                  pl.BlockSpec((N_META, bn), lambda j, i: (0, j)),
                  pl.BlockSpec((N_META, bn), lambda j, i: (0, j))],
        out_specs=out_specs,
        out_shape=out_shape,
        scratch_shapes=[pltpu.VMEM((2, _HALO + bm, bn), F32)],
        compiler_params=_params("parallel", "arbitrary"),
        name="up_conv",
    )(hn, w_up, w_up, conv_w, conv_w, conv_b, conv_b, hist_a, hist_v)


def _lam_init(layer):
    return 0.8 - 0.6 * math.exp(-0.3 * layer)


def _trunk(h, prm, side_in, *, nbatch, seq, cfg):
    is_meta = side_in is None
    hn = _entry_norm(h, prm["pre_mix_g"][0], cfg["bm_norm"])
    side_out = []
    for l in range(DEPTH):
        lam_init = _lam_init(l)
        p = _in_proj(hn, prm["w_in"], l, cfg["bm_in"], cfg["bn_in"])
        gk = _gate(hn, prm["w_lr"], prm["w2"], prm["b2"], l, cfg["bm_gate"])
        if is_meta:
            p_prefix = p
            s0 = jnp.zeros((GLA_HEADS, GLA_HEAD_K, GLA_HEAD_V), F32)
            hist_a = hist_v = jnp.zeros((N_META, D_FF), F32)
        else:
            p_prefix, s0, hist_a, hist_v = side_in[l]
        o_da = _diff_attention(p, p_prefix, prm["da_lambda"], prm["da_subln_g"], l, lam_init,
                               nbatch, seq, cfg["tq"], has_prefix=not is_meta)
        o_gla, s_fin = _gla(p, gk, prm["gla_norm_g"], s0, l, nbatch, seq, cfg["chunk"], cfg["nchunks"])
        h, hn = _proj_residual((o_da, o_gla), prm["w_out"], h, prm["post_mix_g"][l], prm["pre_ffn_g"][l], l,
                               cfg["bm_out"], "out_proj")
        ffn = _up_conv(hn, prm["w_up"], prm["conv_w"], prm["conv_b"], hist_a, hist_v, l,
                       cfg["bm_up"], cfg["bn_up"], seq // cfg["bm_up"], emit_u=is_meta)
        if is_meta:
            act, u_a, u_v = ffn
            side_out.append((p, s_fin, u_a, u_v))
        else:
            (act,) = ffn
        g_next = prm["pre_mix_g"][(l + 1) % DEPTH]
        h, hn = _proj_residual((act,), prm["w_down"], h, prm["post_ffn_g"][l], g_next, l, cfg["bm_down"], "down_proj")
    return h, side_out


_META_CFG = dict(bm_norm=N_META, bm_in=N_META, bn_in=1024, bm_gate=N_META, tq=N_META, chunk=N_META, nchunks=1,
                 bm_out=N_META, bm_up=N_META, bn_up=512, bm_down=N_META)
_SEQ_CFG = dict(bm_norm=512, bm_in=1024, bn_in=512, bm_gate=1024, tq=512, chunk=GLA_CHUNK, nchunks=32,
                bm_out=256, bm_up=1024, bn_up=512, bm_down=256)


def kernel(x, meta_tokens, pre_mix_g, w_in, da_lambda, da_subln_g, gla_gate_w2, gla_gate_b, gla_norm_g, w_out,
           post_mix_g, pre_ffn_g, w_up, conv_w, conv_b, w_down, post_ffn_g):
    vec = lambda a: a.astype(F32).reshape(DEPTH, 1, a.shape[-1])
    prm = dict(
        pre_mix_g=vec(pre_mix_g), post_mix_g=vec(post_mix_g), pre_ffn_g=vec(pre_ffn_g), post_ffn_g=vec(post_ffn_g),
        w_in=w_in.astype(BF16),
        w_lr=jnp.pad(w_in[:, :, IN_MAIN:], ((0, 0), (0, 0), (0, LANES - GLA_GATE_RANK))).astype(BF16),
        w2=jnp.pad(gla_gate_w2, ((0, 0), (0, LANES - GLA_GATE_RANK), (0, 0))).astype(BF16),
        b2=vec(gla_gate_b),
        da_lambda=da_lambda.astype(F32), da_subln_g=vec(da_subln_g), gla_norm_g=vec(gla_norm_g),
        w_out=w_out.astype(BF16), w_up=w_up.astype(BF16), w_down=w_down.astype(BF16),
        conv_w=conv_w.astype(F32), conv_b=vec(conv_b),
    )
    _, side = _trunk(meta_tokens.astype(F32), prm, None, nbatch=1, seq=N_META, cfg=_META_CFG)
    h, _ = _trunk(x.astype(F32).reshape(BATCH * SEQ, D_MODEL), prm, side, nbatch=BATCH, seq=SEQ, cfg=_SEQ_CFG)
    return h.reshape(BATCH, SEQ, D_MODEL).astype(x.dtype)
```

```python
import functools
import math

import jax
import jax.numpy as jnp
from jax import lax
from jax.experimental import pallas as pl
from jax.experimental.pallas import tpu as pltpu

D_MODEL = 2048
BATCH = 4
SEQ = 2048
DEPTH = 4
N_META = 16

DA_HEADS = 8
DA_HEAD_DIM = 64
DA_HEAD_WIDTH = 2 * DA_HEAD_DIM
DA_WIDTH = DA_HEADS * DA_HEAD_WIDTH
GLA_HEADS = 4
GLA_WIDTH = 1024
GLA_HEAD_V = 256
GLA_KEY_WIDTH = 512
GLA_HEAD_K = 128
GLA_GATE_RANK = 16
GLA_GATE_NORM = 16.0
GLA_CHUNK = 64
D_FF = 5632
EPS = 1e-6

COL_Q_DA = 0
COL_K_DA = DA_WIDTH
COL_V_DA = 2 * DA_WIDTH
COL_Q_G = 3 * DA_WIDTH
COL_K_G = COL_Q_G + GLA_KEY_WIDTH
COL_V_G = COL_K_G + GLA_KEY_WIDTH
COL_G_G = COL_V_G + GLA_WIDTH
COL_LR = COL_G_G + GLA_WIDTH
IN_MAIN = COL_LR
LANES = 128

VMEM_LIMIT = 56 * 1024 * 1024

F32 = jnp.float32
BF16 = jnp.bfloat16

_NT = (((1,), (1,)), ((), ()))
_TN = (((0,), (0,)), ((), ()))
_LOG2E = math.log2(math.e)


def _params(*sem):
    return pltpu.CompilerParams(dimension_semantics=sem, vmem_limit_bytes=VMEM_LIMIT)


def _rms(x, g):
    return x * lax.rsqrt(jnp.mean(x * x, axis=-1, keepdims=True) + EPS) * g


def _silu(x):
    return x * (1.0 / (1.0 + jnp.exp(-x)))


def _norm_kernel(x_ref, g_ref, o_ref):
    o_ref[...] = _rms(x_ref[...], g_ref[...]).astype(o_ref.dtype)


def _entry_norm(x, g, bm):
    m = x.shape[0]
    return pl.pallas_call(
        _norm_kernel,
        grid=(m // bm,),
        in_specs=[pl.BlockSpec((bm, D_MODEL), lambda i: (i, 0)),
                  pl.BlockSpec((1, D_MODEL), lambda i: (0, 0))],
        out_specs=pl.BlockSpec((bm, D_MODEL), lambda i: (i, 0)),
        out_shape=jax.ShapeDtypeStruct((m, D_MODEL), BF16),
        compiler_params=_params("parallel"),
        name="entry_norm",
    )(x, g)


def _mm_kernel(x_ref, w_ref, o_ref):
    o_ref[...] = jnp.dot(x_ref[...], w_ref[...], preferred_element_type=F32).astype(o_ref.dtype)


def _in_proj(hn, w_main, layer, bm, bn):
    m = hn.shape[0]
    return pl.pallas_call(
        _mm_kernel,
        grid=(m // bm, IN_MAIN // bn),
        in_specs=[pl.BlockSpec((bm, D_MODEL), lambda i, j: (i, 0)),
                  pl.BlockSpec((None, D_MODEL, bn), lambda i, j: (layer, 0, j))],
        out_specs=pl.BlockSpec((bm, bn), lambda i, j: (i, j)),
        out_shape=jax.ShapeDtypeStruct((m, IN_MAIN), BF16),
        compiler_params=_params("parallel", "arbitrary"),
        name="in_proj",
    )(hn, w_main)


def _gate_kernel(x_ref, wlr_ref, w2_ref, b2_ref, o_ref):
    lr = jnp.dot(x_ref[...], wlr_ref[...], preferred_element_type=F32)
    z = jnp.dot(lr.astype(BF16), w2_ref[...], preferred_element_type=F32) + b2_ref[...]
    log_sig = jnp.minimum(z, 0.0) - jnp.log1p(jnp.exp(-jnp.abs(z)))
    o_ref[...] = log_sig * (1.0 / GLA_GATE_NORM)


def _gate(hn, w_lr, w2, b2, layer, bm):
    m = hn.shape[0]
    return pl.pallas_call(
        _gate_kernel,
        grid=(m // bm,),
        in_specs=[pl.BlockSpec((bm, D_MODEL), lambda i: (i, 0)),
                  pl.BlockSpec((None, D_MODEL, LANES), lambda i: (layer, 0, 0)),
                  pl.BlockSpec((None, LANES, GLA_KEY_WIDTH), lambda i: (layer, 0, 0)),
                  pl.BlockSpec((None, 1, GLA_KEY_WIDTH), lambda i: (layer, 0, 0))],
        out_specs=pl.BlockSpec((bm, GLA_KEY_WIDTH), lambda i: (i, 0)),
        out_shape=jax.ShapeDtypeStruct((m, GLA_KEY_WIDTH), F32),
        compiler_params=_params("parallel"),
        name="gla_gate",
    )(hn, w_lr, w2, b2)


def _da_kernel(lam_ref, g_ref, q_ref, k_ref, v_ref, kp_ref, vp_ref, o_ref, vx_scr, vpx_scr,
               *, tq, heads, lam_init, has_prefix):
    qi = pl.program_id(2)
    seq = v_ref.shape[0]
    hw = DA_HEAD_WIDTH
    head_lanes = [slice(g * hw, (g + 1) * hw) for g in range(heads)]

    @pl.when(qi == 0)
    def _():
        for g, lanes in enumerate(head_lanes):
            vx_scr[g, :, :hw] = v_ref[:, lanes]
            vx_scr[g, :, hw:] = jnp.ones((seq, hw), BF16)
            if has_prefix:
                vpx_scr[g, :, :hw] = vp_ref[:, lanes]
                vpx_scr[g, :, hw:] = jnp.ones((N_META, hw), BF16)

    lane = lax.broadcasted_iota(jnp.int32, (1, hw), 1)
    causal = (lax.broadcasted_iota(jnp.int32, (tq, tq), 0) >= lax.broadcasted_iota(jnp.int32, (tq, tq), 1))
    lamv = lam_ref[...]
    lam = (jnp.exp(jnp.sum(lamv[0:1] * lamv[1:2], axis=-1, keepdims=True))
           - jnp.exp(jnp.sum(lamv[2:3] * lamv[3:4], axis=-1, keepdims=True)) + lam_init)

    def attend(n_below):
        kv_len = (n_below + 1) * tq
        for g, lanes in enumerate(head_lanes):
            q = q_ref[:, lanes].astype(F32) * (DA_HEAD_DIM ** -0.5 * _LOG2E)
            comps = []
            for qc in (jnp.where(lane < DA_HEAD_DIM, q, 0.0).astype(BF16),
                       jnp.where(lane >= DA_HEAD_DIM, q, 0.0).astype(BF16)):
                s = lax.dot_general(qc, k_ref[:kv_len, lanes], _NT, preferred_element_type=F32)
                parts = [jnp.where(causal, s[:, n_below * tq:], -jnp.inf)]
                if n_below:
                    parts.insert(0, s[:, :n_below * tq])
                m = functools.reduce(jnp.maximum, [jnp.max(part, axis=-1, keepdims=True) for part in parts])
                if has_prefix:
                    sp = lax.dot_general(qc, kp_ref[:, lanes], _NT, preferred_element_type=F32)
                    m = jnp.maximum(m, jnp.max(sp, axis=-1, keepdims=True))
                p = [jnp.exp2(part - m).astype(BF16) for part in parts]
                p = p[0] if len(p) == 1 else jnp.concatenate(p, axis=1)
                acc = jnp.dot(p, vx_scr[g, :kv_len, :], preferred_element_type=F32)
                if has_prefix:
                    acc = acc + jnp.dot(jnp.exp2(sp - m).astype(BF16), vpx_scr[g], preferred_element_type=F32)
                comps.append(acc[:, :hw] / acc[:, hw:])
            o = comps[0] - lam * comps[1]
            o_ref[:, lanes] = (_rms(o, g_ref[...]) * (1.0 - lam_init)).astype(o_ref.dtype)

    for n in range(seq // tq):
        pl.when(qi == n)(functools.partial(attend, n))


def _diff_attention(p, p_prefix, lam, subln_g, layer, lam_init, nbatch, seq, tq, heads, has_prefix):
    nq = seq // tq
    width = heads * DA_HEAD_WIDTH
    kcol = COL_K_DA // width
    vcol = COL_V_DA // width
    kernel = functools.partial(_da_kernel, tq=tq, heads=heads, lam_init=lam_init, has_prefix=has_prefix)
    return pl.pallas_call(
        kernel,
        grid=(nbatch, DA_HEADS // heads, nq),
        in_specs=[pl.BlockSpec((None, 4, DA_HEAD_DIM), lambda b, h, i: (layer, 0, 0)),
                  pl.BlockSpec((None, 1, DA_HEAD_WIDTH), lambda b, h, i: (layer, 0, 0)),
                  pl.BlockSpec((tq, width), lambda b, h, i: (b * nq + i, h)),
                  pl.BlockSpec((seq, width), lambda b, h, i: (b, kcol + h)),
                  pl.BlockSpec((seq, width), lambda b, h, i: (b, vcol + h)),
                  pl.BlockSpec((N_META, width), lambda b, h, i: (0, kcol + h)),
                  pl.BlockSpec((N_META, width), lambda b, h, i: (0, vcol + h))],
        out_specs=pl.BlockSpec((tq, width), lambda b, h, i: (b * nq + i, h)),
        out_shape=jax.ShapeDtypeStruct((nbatch * seq, DA_WIDTH), BF16),
        scratch_shapes=[pltpu.VMEM((heads, seq, 2 * DA_HEAD_WIDTH), BF16),
                        pltpu.VMEM((heads, N_META, 2 * DA_HEAD_WIDTH), BF16)],
        compiler_params=_params("parallel", "parallel", "arbitrary"),
        name="diff_attention",
    )(lam, subln_g, p, p, p, p_prefix, p_prefix)


def _split_bf16(x):
    hi = x.astype(BF16)
    lo = (x - hi.astype(F32)).astype(BF16)
    return hi, lo


def _gla_kernel(q_ref, k_ref, v_ref, go_ref, gk_ref, ng_ref, s0_ref, o_ref, sfin_ref, s_scr,
                *, chunk, nchunks):
    blk = pl.program_id(2)

    @pl.when(blk == 0)
    def _():
        s_scr[...] = s0_ref[...]

    row = lax.broadcasted_iota(jnp.int32, (chunk, chunk), 0)
    col = lax.broadcasted_iota(jnp.int32, (chunk, chunk), 1)
    tril = row >= col
    tril_bf = tril.astype(F32).astype(BF16)
    ones = jnp.ones((chunk, GLA_HEAD_V), BF16)
    ng = ng_ref[...]

    for c in range(nchunks):
        sl = pl.ds(c * chunk, chunk)
        gk_hi, gk_lo = _split_bf16(gk_ref[sl, :])
        b = (jnp.dot(tril_bf, gk_hi, preferred_element_type=F32)
             + jnp.dot(tril_bf, gk_lo, preferred_element_type=F32))
        b_last_col = (lax.dot_general(gk_hi, ones, _TN, preferred_element_type=F32)
                      + lax.dot_general(gk_lo, ones, _TN, preferred_element_type=F32))
        b_last = b[chunk - 1:chunk, :]
        q = q_ref[sl, :].astype(F32) * (GLA_HEAD_K ** -0.5)
        k = k_ref[sl, :].astype(F32)
        v = v_ref[sl, :]
        q_in = (q * jnp.exp(b)).astype(BF16)
        k_in = (k * jnp.exp(-b)).astype(BF16)
        k_dec = (k * jnp.exp(b_last - b)).astype(BF16)
        a = lax.dot_general(q_in, k_in, _NT, preferred_element_type=F32)
        a = jnp.where(tril, a, 0.0)
        s_prev = s_scr[...]
        o = (jnp.dot(a.astype(BF16), v, preferred_element_type=F32)
             + jnp.dot(q_in, s_prev.astype(BF16), preferred_element_type=F32))
        s_scr[...] = jnp.exp(b_last_col) * s_prev + lax.dot_general(k_dec, v, _TN, preferred_element_type=F32)
        go = go_ref[sl, :].astype(F32)
        o_ref[sl, :] = (_rms(o, ng) * _silu(go)).astype(o_ref.dtype)

    @pl.when(blk == pl.num_programs(2) - 1)
    def _():
        sfin_ref[...] = s_scr[...]


def _gla(p, gk, norm_g, s0, layer, nbatch, seq, chunk, nchunks):
    blk_rows = chunk * nchunks
    nblk = seq // blk_rows
    qcol = COL_Q_G // GLA_HEAD_K
    kcol = COL_K_G // GLA_HEAD_K
    vcol = COL_V_G // GLA_HEAD_V
    gcol = COL_G_G // GLA_HEAD_V
    kernel = functools.partial(_gla_kernel, chunk=chunk, nchunks=nchunks)
    return pl.pallas_call(
        kernel,
        grid=(nbatch, GLA_HEADS, nblk),
        in_specs=[pl.BlockSpec((blk_rows, GLA_HEAD_K), lambda b, h, i: (b * nblk + i, qcol + h)),
                  pl.BlockSpec((blk_rows, GLA_HEAD_K), lambda b, h, i: (b * nblk + i, kcol + h)),
                  pl.BlockSpec((blk_rows, GLA_HEAD_V), lambda b, h, i: (b * nblk + i, vcol + h)),
                  pl.BlockSpec((blk_rows, GLA_HEAD_V), lambda b, h, i: (b * nblk + i, gcol + h)),
                  pl.BlockSpec((blk_rows, GLA_HEAD_K), lambda b, h, i: (b * nblk + i, h)),
                  pl.BlockSpec((None, 1, GLA_HEAD_V), lambda b, h, i: (layer, 0, 0)),
                  pl.BlockSpec((None, GLA_HEAD_K, GLA_HEAD_V), lambda b, h, i: (h, 0, 0))],
        out_specs=[pl.BlockSpec((blk_rows, GLA_HEAD_V), lambda b, h, i: (b * nblk + i, h)),
                   pl.BlockSpec((None, GLA_HEAD_K, GLA_HEAD_V), lambda b, h, i: (b * GLA_HEADS + h, 0, 0))],
        out_shape=[jax.ShapeDtypeStruct((nbatch * seq, GLA_WIDTH), BF16),
                   jax.ShapeDtypeStruct((nbatch * GLA_HEADS, GLA_HEAD_K, GLA_HEAD_V), F32)],
        scratch_shapes=[pltpu.VMEM((GLA_HEAD_K, GLA_HEAD_V), F32)],
        compiler_params=_params("parallel", "parallel", "arbitrary"),
        name="gla",
    )(p, p, p, p, gk, norm_g, s0)


def _deferred(step, nblocks, produce, finish):
    def emit(produce_parts, finish_parts):
        for i in range(max(len(produce_parts), len(finish_parts))):
            for parts in (produce_parts, finish_parts):
                if i < len(parts):
                    parts[i]()

    @pl.when(step == 0)
    def _():
        emit(produce(0), [])

    for slot in range(2):
        @pl.when(jnp.logical_and(jnp.logical_and(step > 0, step < nblocks), step % 2 == slot))
        def _(slot=slot):
            emit(produce(slot), finish(1 - slot))

    @pl.when(step == nblocks)
    def _():
        emit([], finish((nblocks - 1) % 2))


def _cur_block(s, nblocks):
    return jnp.minimum(s, nblocks - 1)


def _prev_block(s):
    return jnp.maximum(s - 1, 0)


_PROJ_PARTS = 4


def _residual_rows(rows, y_ref, h_ref, gpost_ref, gnext_ref, ho_ref, hn_ref):
    h = h_ref[rows, :] + _rms(y_ref[rows, :], gpost_ref[...])
    ho_ref[rows, :] = h
    hn_ref[rows, :] = _rms(h, gnext_ref[...]).astype(hn_ref.dtype)


def _proj_residual_kernel(*refs, nblocks, k_splits):
    nx = len(k_splits)
    x_refs, (w_ref, h_ref, gpost_ref, gnext_ref, ho_ref, hn_ref, y_scr) = refs[:nx], refs[nx:]
    bm, n = h_ref.shape
    nparts = _PROJ_PARTS if bm % (8 * _PROJ_PARTS) == 0 else 1

    def produce(slot):
        def part(c):
            cols = slice(c * (n // nparts), (c + 1) * (n // nparts))
            y, lo = None, 0
            for x_ref, k in zip(x_refs, k_splits):
                t = jnp.dot(x_ref[...], w_ref[lo:lo + k, cols], preferred_element_type=F32)
                y = t if y is None else y + t
                lo += k
            y_scr[slot, :, cols] = y
        return [functools.partial(part, c) for c in range(nparts)]

    def finish(slot):
        def part(r):
            rows = slice(r * (bm // nparts), (r + 1) * (bm // nparts))
            _residual_rows(rows, y_scr.at[slot], h_ref, gpost_ref, gnext_ref, ho_ref, hn_ref)
        return [functools.partial(part, r) for r in range(nparts)]

    _deferred(pl.program_id(0), nblocks, produce, finish)


def _proj_residual(xs, w, h, g_post, g_next, layer, bm, name):
    m = h.shape[0]
    nblocks = m // bm
    k_splits = tuple(x.shape[1] for x in xs)
    k_total = sum(k_splits)
    cur = lambda s: (_cur_block(s, nblocks), 0)
    prev = lambda s: (_prev_block(s), 0)
    const = lambda s: (0, 0)
    kernel = functools.partial(_proj_residual_kernel, nblocks=nblocks, k_splits=k_splits)
    return pl.pallas_call(
        kernel,
        grid=(nblocks + 1,),
        in_specs=[pl.BlockSpec((bm, k), cur) for k in k_splits] + [
            pl.BlockSpec((None, k_total, D_MODEL), lambda s: (layer, 0, 0), pipeline_mode=pl.Buffered(1)),
            pl.BlockSpec((bm, D_MODEL), prev),
            pl.BlockSpec((1, D_MODEL), const),
            pl.BlockSpec((1, D_MODEL), const)],
        out_specs=[pl.BlockSpec((bm, D_MODEL), prev), pl.BlockSpec((bm, D_MODEL), prev)],
        out_shape=[jax.ShapeDtypeStruct((m, D_MODEL), F32), jax.ShapeDtypeStruct((m, D_MODEL), BF16)],
        scratch_shapes=[pltpu.VMEM((2, bm, D_MODEL), F32)],
        compiler_params=_params("arbitrary"),
        name=name,
    )(*xs, w, h, g_post, g_next)


_HALO = 8
_UP_SUB_ROWS = 256


def _up_conv_kernel(x_ref, wa_ref, wv_ref, cwa_ref, cwv_ref, cba_ref, cbv_ref, ha_ref, hv_ref, *rest,
                    bm, sub_rows, blocks_per_seq, emit_u):
    if emit_u:
        o_ref, ua_ref, uv_ref, ubuf = rest
    else:
        o_ref, ubuf = rest
    i = pl.program_id(1)
    first = (i % blocks_per_seq) == 0
    halves = ((wa_ref, cwa_ref, cba_ref, ha_ref), (wv_ref, cwv_ref, cbv_ref, hv_ref))
    for idx, (_, _, _, h_ref) in enumerate(halves):
        buf = ubuf.at[idx]

        @pl.when(first)
        def _():
            buf[_HALO - 2:_HALO, :] = h_ref[N_META - 2:N_META, :]

        @pl.when(jnp.logical_not(first))
        def _():
            buf[_HALO - 2:_HALO, :] = buf[_HALO + bm - 2:_HALO + bm, :]

    sb = min(bm, sub_rows)
    for r in range(bm // sb):
        rows = slice(r * sb, (r + 1) * sb)
        lo = _HALO + r * sb
        x = x_ref[rows, :]
        conv = []
        for idx, (w_ref, cw_ref, cb_ref, _) in enumerate(halves):
            u = jnp.dot(x, w_ref[...], preferred_element_type=F32)
            buf = ubuf.at[idx]
            buf[lo:lo + sb, :] = u
            cw = cw_ref[...]
            conv.append(cb_ref[...]
                        + cw[0:1] * buf[lo - 2:lo - 2 + sb, :]
                        + cw[1:2] * buf[lo - 1:lo - 1 + sb, :]
                        + cw[2:3] * u)
            if emit_u:
                (ua_ref, uv_ref)[idx][rows, :] = u
        o_ref[rows, :] = (_silu(conv[0]) * conv[1]).astype(o_ref.dtype)


def _up_conv(hn, w_up, conv_w, conv_b, hist_a, hist_v, layer, bm, bn, blocks_per_seq, emit_u):
    m = hn.shape[0]
    nj = D_FF // bn
    kernel = functools.partial(_up_conv_kernel, bm=bm, sub_rows=_UP_SUB_ROWS, blocks_per_seq=blocks_per_seq,
                               emit_u=emit_u)
    out_specs = [pl.BlockSpec((bm, bn), lambda j, i: (i, j))]
    out_shape = [jax.ShapeDtypeStruct((m, D_FF), BF16)]
    if emit_u:
        out_specs += [pl.BlockSpec((bm, bn), lambda j, i: (i, j))] * 2
        out_shape += [jax.ShapeDtypeStruct((m, D_FF), F32)] * 2
    return pl.pallas_call(
        kernel,
        grid=(nj, m // bm),
        in_specs=[pl.BlockSpec((bm, D_MODEL), lambda j, i: (i, 0)),
                  pl.BlockSpec((None, D_MODEL, bn), lambda j, i: (layer, 0, j)),
                  pl.BlockSpec((None, D_MODEL, bn), lambda j, i: (layer, 0, j + nj)),
                  pl.BlockSpec((None, 3, bn), lambda j, i: (layer, 0, j)),
                  pl.BlockSpec((None, 3, bn), lambda j, i: (layer, 0, j + nj)),
                  pl.BlockSpec((None, 1, bn), lambda j, i: (layer, 0, j)),
                  pl.BlockSpec((None, 1, bn), lambda j, i: (layer, 0, j + nj)),
                  pl.BlockSpec((N_META, bn), lambda j, i: (0, j)),
                  pl.BlockSpec((N_META, bn), lambda j, i: (0, j))],
        out_specs=out_specs,
        out_shape=out_shape,
        scratch_shapes=[pltpu.VMEM((2, _HALO + bm, bn), F32)],
        compiler_params=_params("parallel", "arbitrary"),
        name="up_conv",
    )(hn, w_up, w_up, conv_w, conv_w, conv_b, conv_b, hist_a, hist_v)


def _lam_init(layer):
    return 0.8 - 0.6 * math.exp(-0.3 * layer)


def _trunk(h, prm, side_in, *, nbatch, seq, cfg):
    is_meta = side_in is None
    hn = _entry_norm(h, prm["pre_mix_g"][0], cfg["bm_norm"])
    side_out = []
    for l in range(DEPTH):
        lam_init = _lam_init(l)
        p = _in_proj(hn, prm["w_in"], l, cfg["bm_in"], cfg["bn_in"])
        gk = _gate(hn, prm["w_lr"], prm["w2"], prm["b2"], l, cfg["bm_gate"])
        if is_meta:
            p_prefix = p
            s0 = jnp.zeros((GLA_HEADS, GLA_HEAD_K, GLA_HEAD_V), F32)
            hist_a = hist_v = jnp.zeros((N_META, D_FF), F32)
        else:
            p_prefix, s0, hist_a, hist_v = side_in[l]
        o_da = _diff_attention(p, p_prefix, prm["da_lambda"], prm["da_subln_g"], l, lam_init,
                               nbatch, seq, cfg["tq"], cfg["da_heads"], has_prefix=not is_meta)
        o_gla, s_fin = _gla(p, gk, prm["gla_norm_g"], s0, l, nbatch, seq, cfg["chunk"], cfg["nchunks"])
        h, hn = _proj_residual((o_da, o_gla), prm["w_out"], h, prm["post_mix_g"][l], prm["pre_ffn_g"][l], l,
                               cfg["bm_out"], "out_proj")
        ffn = _up_conv(hn, prm["w_up"], prm["conv_w"], prm["conv_b"], hist_a, hist_v, l,
                       cfg["bm_up"], cfg["bn_up"], seq // cfg["bm_up"], emit_u=is_meta)
        if is_meta:
            act, u_a, u_v = ffn
            side_out.append((p, s_fin, u_a, u_v))
        else:
            (act,) = ffn
        g_next = prm["pre_mix_g"][(l + 1) % DEPTH]
        h, hn = _proj_residual((act,), prm["w_down"], h, prm["post_ffn_g"][l], g_next, l, cfg["bm_down"], "down_proj")
    return h, side_out


_META_CFG = dict(bm_norm=N_META, bm_in=N_META, bn_in=1024, bm_gate=N_META, tq=N_META, da_heads=2, chunk=N_META, nchunks=1,
                 bm_out=N_META, bm_up=N_META, bn_up=512, bm_down=N_META)
_SEQ_CFG = dict(bm_norm=512, bm_in=1024, bn_in=1024, bm_gate=1024, tq=512, da_heads=4, chunk=GLA_CHUNK, nchunks=32,
                bm_out=256, bm_up=1024, bn_up=512, bm_down=256)


def kernel(x, meta_tokens, pre_mix_g, w_in, da_lambda, da_subln_g, gla_gate_w2, gla_gate_b, gla_norm_g, w_out,
           post_mix_g, pre_ffn_g, w_up, conv_w, conv_b, w_down, post_ffn_g):
    vec = lambda a: a.astype(F32).reshape(DEPTH, 1, a.shape[-1])
    prm = dict(
        pre_mix_g=vec(pre_mix_g), post_mix_g=vec(post_mix_g), pre_ffn_g=vec(pre_ffn_g), post_ffn_g=vec(post_ffn_g),
        w_in=w_in.astype(BF16),
        w_lr=jnp.pad(w_in[:, :, IN_MAIN:], ((0, 0), (0, 0), (0, LANES - GLA_GATE_RANK))).astype(BF16),
        w2=jnp.pad(gla_gate_w2, ((0, 0), (0, LANES - GLA_GATE_RANK), (0, 0))).astype(BF16),
        b2=vec(gla_gate_b),
        da_lambda=da_lambda.astype(F32), da_subln_g=vec(da_subln_g), gla_norm_g=vec(gla_norm_g),
        w_out=w_out.astype(BF16), w_up=w_up.astype(BF16), w_down=w_down.astype(BF16),
        conv_w=conv_w.astype(F32), conv_b=vec(conv_b),
    )
    _, side = _trunk(meta_tokens.astype(F32), prm, None, nbatch=1, seq=N_META, cfg=_META_CFG)
    h, _ = _trunk(x.astype(F32).reshape(BATCH * SEQ, D_MODEL), prm, side, nbatch=BATCH, seq=SEQ, cfg=_SEQ_CFG)
    return h.reshape(BATCH, SEQ, D_MODEL).astype(x.dtype)
```

```python
import functools
import math

import jax
import jax.numpy as jnp
from jax import lax
from jax.experimental import pallas as pl
from jax.experimental.pallas import tpu as pltpu

D_MODEL = 2048
BATCH = 4
SEQ = 2048
DEPTH = 4
N_META = 16

DA_HEADS = 8
DA_HEAD_DIM = 64
DA_HEAD_WIDTH = 2 * DA_HEAD_DIM
DA_WIDTH = DA_HEADS * DA_HEAD_WIDTH
GLA_HEADS = 4
GLA_WIDTH = 1024
GLA_HEAD_V = 256
GLA_KEY_WIDTH = 512
GLA_HEAD_K = 128
GLA_GATE_RANK = 16
GLA_GATE_NORM = 16.0
GLA_CHUNK = 64
D_FF = 5632
EPS = 1e-6

COL_Q_DA = 0
COL_K_DA = DA_WIDTH
COL_V_DA = 2 * DA_WIDTH
COL_Q_G = 3 * DA_WIDTH
COL_K_G = COL_Q_G + GLA_KEY_WIDTH
COL_V_G = COL_K_G + GLA_KEY_WIDTH
COL_G_G = COL_V_G + GLA_WIDTH
COL_LR = COL_G_G + GLA_WIDTH
IN_MAIN = COL_LR
LANES = 128

VMEM_LIMIT = 56 * 1024 * 1024

F32 = jnp.float32
BF16 = jnp.bfloat16

_NT = (((1,), (1,)), ((), ()))
_TN = (((0,), (0,)), ((), ()))
_LOG2E = math.log2(math.e)


def _params(*sem):
    return pltpu.CompilerParams(dimension_semantics=sem, vmem_limit_bytes=VMEM_LIMIT)


def _rms(x, g):
    return x * lax.rsqrt(jnp.mean(x * x, axis=-1, keepdims=True) + EPS) * g


def _silu(x):
    return x * (1.0 / (1.0 + jnp.exp(-x)))


def _norm_kernel(x_ref, g_ref, o_ref):
    o_ref[...] = _rms(x_ref[...], g_ref[...]).astype(o_ref.dtype)


def _entry_norm(x, g, bm):
    m = x.shape[0]
    return pl.pallas_call(
        _norm_kernel,
        grid=(m // bm,),
        in_specs=[pl.BlockSpec((bm, D_MODEL), lambda i: (i, 0)),
                  pl.BlockSpec((1, D_MODEL), lambda i: (0, 0))],
        out_specs=pl.BlockSpec((bm, D_MODEL), lambda i: (i, 0)),
        out_shape=jax.ShapeDtypeStruct((m, D_MODEL), BF16),
        compiler_params=_params("parallel"),
        name="entry_norm",
    )(x, g)


def _in_proj_kernel(x_ref, xm_ref, w_ref, o_ref, om_ref, wbf_scr):
    @pl.when(pl.program_id(1) == 0)
    def _():
        wbf_scr[...] = w_ref[...].astype(BF16)
        om_ref[...] = jnp.dot(xm_ref[...], wbf_scr[...], preferred_element_type=F32).astype(om_ref.dtype)

    o_ref[...] = jnp.dot(x_ref[...], wbf_scr[...], preferred_element_type=F32).astype(o_ref.dtype)


def _in_proj(hn, hn_meta, w_in, layer, bm, bn):
    m = hn.shape[0]
    return pl.pallas_call(
        _in_proj_kernel,
        grid=(IN_MAIN // bn, m // bm),
        in_specs=[pl.BlockSpec((bm, D_MODEL), lambda j, i: (i, 0)),
                  pl.BlockSpec((N_META, D_MODEL), lambda j, i: (0, 0)),
                  pl.BlockSpec((None, D_MODEL, bn), lambda j, i: (layer, 0, j))],
        out_specs=[pl.BlockSpec((bm, bn), lambda j, i: (i, j)),
                   pl.BlockSpec((N_META, bn), lambda j, i: (0, j))],
        out_shape=[jax.ShapeDtypeStruct((m, IN_MAIN), BF16),
                   jax.ShapeDtypeStruct((N_META, IN_MAIN), BF16)],
        scratch_shapes=[pltpu.VMEM((D_MODEL, bn), BF16)],
        compiler_params=_params("parallel", "arbitrary"),
        name="in_proj",
    )(hn, hn_meta, w_in)


def _gate_kernel(x_ref, wlr_ref, w2_ref, b2_ref, o_ref):
    lr = jnp.dot(x_ref[...], wlr_ref[...], preferred_element_type=F32)
    z = jnp.dot(lr.astype(BF16), w2_ref[...], preferred_element_type=F32) + b2_ref[...]
    log_sig = jnp.minimum(z, 0.0) - jnp.log1p(jnp.exp(-jnp.abs(z)))
    o_ref[...] = log_sig * (1.0 / GLA_GATE_NORM)


def _gate(hn, w_lr, w2, b2, layer, bm):
    m = hn.shape[0]
    return pl.pallas_call(
        _gate_kernel,
        grid=(m // bm,),
        in_specs=[pl.BlockSpec((bm, D_MODEL), lambda i: (i, 0)),
                  pl.BlockSpec((None, D_MODEL, LANES), lambda i: (layer, 0, 0)),
                  pl.BlockSpec((None, LANES, GLA_KEY_WIDTH), lambda i: (layer, 0, 0)),
                  pl.BlockSpec((None, 1, GLA_KEY_WIDTH), lambda i: (layer, 0, 0))],
        out_specs=pl.BlockSpec((bm, GLA_KEY_WIDTH), lambda i: (i, 0)),
        out_shape=jax.ShapeDtypeStruct((m, GLA_KEY_WIDTH), F32),
        compiler_params=_params("parallel"),
        name="gla_gate",
    )(hn, w_lr, w2, b2)


def _da_kernel(lam_ref, g_ref, q_ref, k_ref, v_ref, kp_ref, vp_ref, o_ref, vx_scr, vpx_scr,
               *, tq, heads, lam_init, has_prefix):
    qi = pl.program_id(2)
    seq = v_ref.shape[0]
    hw = DA_HEAD_WIDTH
    head_lanes = [slice(g * hw, (g + 1) * hw) for g in range(heads)]

    @pl.when(qi == 0)
    def _():
        for g, lanes in enumerate(head_lanes):
            vx_scr[g, :, :hw] = v_ref[:, lanes]
            vx_scr[g, :, hw:] = jnp.ones((seq, hw), BF16)
            if has_prefix:
                vpx_scr[g, :, :hw] = vp_ref[:, lanes]
                vpx_scr[g, :, hw:] = jnp.ones((N_META, hw), BF16)

    lane = lax.broadcasted_iota(jnp.int32, (1, hw), 1)
    causal = (lax.broadcasted_iota(jnp.int32, (tq, tq), 0) >= lax.broadcasted_iota(jnp.int32, (tq, tq), 1))
    lamv = lam_ref[...]
    lam = (jnp.exp(jnp.sum(lamv[0:1] * lamv[1:2], axis=-1, keepdims=True))
           - jnp.exp(jnp.sum(lamv[2:3] * lamv[3:4], axis=-1, keepdims=True)) + lam_init)

    def attend(n_below):
        kv_len = (n_below + 1) * tq
        for g, lanes in enumerate(head_lanes):
            q = q_ref[:, lanes].astype(F32) * (DA_HEAD_DIM ** -0.5 * _LOG2E)
            comps = []
            for qc in (jnp.where(lane < DA_HEAD_DIM, q, 0.0).astype(BF16),
                       jnp.where(lane >= DA_HEAD_DIM, q, 0.0).astype(BF16)):
                s = lax.dot_general(qc, k_ref[:kv_len, lanes], _NT, preferred_element_type=F32)
                parts = [jnp.where(causal, s[:, n_below * tq:], -jnp.inf)]
                if n_below:
                    parts.insert(0, s[:, :n_below * tq])
                m = functools.reduce(jnp.maximum, [jnp.max(part, axis=-1, keepdims=True) for part in parts])
                if has_prefix:
                    sp = lax.dot_general(qc, kp_ref[:, lanes], _NT, preferred_element_type=F32)
                    m = jnp.maximum(m, jnp.max(sp, axis=-1, keepdims=True))
                p = [jnp.exp2(part - m).astype(BF16) for part in parts]
                p = p[0] if len(p) == 1 else jnp.concatenate(p, axis=1)
                acc = jnp.dot(p, vx_scr[g, :kv_len, :], preferred_element_type=F32)
                if has_prefix:
                    acc = acc + jnp.dot(jnp.exp2(sp - m).astype(BF16), vpx_scr[g], preferred_element_type=F32)
                comps.append(acc[:, :hw] / acc[:, hw:])
            o = comps[0] - lam * comps[1]
            o_ref[:, lanes] = (_rms(o, g_ref[...]) * (1.0 - lam_init)).astype(o_ref.dtype)

    for n in range(seq // tq):
        pl.when(qi == n)(functools.partial(attend, n))


def _diff_attention(p, p_prefix, lam, subln_g, layer, lam_init, nbatch, seq, tq, heads, has_prefix):
    nq = seq // tq
    width = heads * DA_HEAD_WIDTH
    kcol = COL_K_DA // width
    vcol = COL_V_DA // width
    kernel = functools.partial(_da_kernel, tq=tq, heads=heads, lam_init=lam_init, has_prefix=has_prefix)
    return pl.pallas_call(
        kernel,
        grid=(nbatch, DA_HEADS // heads, nq),
        in_specs=[pl.BlockSpec((None, 4, DA_HEAD_DIM), lambda b, h, i: (layer, 0, 0)),
                  pl.BlockSpec((None, 1, DA_HEAD_WIDTH), lambda b, h, i: (layer, 0, 0)),
                  pl.BlockSpec((tq, width), lambda b, h, i: (b * nq + i, h)),
                  pl.BlockSpec((seq, width), lambda b, h, i: (b, kcol + h)),
                  pl.BlockSpec((seq, width), lambda b, h, i: (b, vcol + h)),
                  pl.BlockSpec((N_META, width), lambda b, h, i: (0, kcol + h)),
                  pl.BlockSpec((N_META, width), lambda b, h, i: (0, vcol + h))],
        out_specs=pl.BlockSpec((tq, width), lambda b, h, i: (b * nq + i, h)),
        out_shape=jax.ShapeDtypeStruct((nbatch * seq, DA_WIDTH), BF16),
        scratch_shapes=[pltpu.VMEM((heads, seq, 2 * DA_HEAD_WIDTH), BF16),
                        pltpu.VMEM((heads, N_META, 2 * DA_HEAD_WIDTH), BF16)],
        compiler_params=_params("parallel", "parallel", "arbitrary"),
        name="diff_attention",
    )(lam, subln_g, p, p, p, p_prefix, p_prefix)


def _split_bf16(x):
    hi = x.astype(BF16)
    lo = (x - hi.astype(F32)).astype(BF16)
    return hi, lo


def _gla_kernel(q_ref, k_ref, v_ref, go_ref, gk_ref, ng_ref, s0_ref, o_ref, sfin_ref, s_scr,
                *, chunk, nchunks):
    blk = pl.program_id(2)

    @pl.when(blk == 0)
    def _():
        s_scr[...] = s0_ref[...]

    row = lax.broadcasted_iota(jnp.int32, (chunk, chunk), 0)
    col = lax.broadcasted_iota(jnp.int32, (chunk, chunk), 1)
    tril = row >= col
    tril_bf = tril.astype(F32).astype(BF16)
    ones = jnp.ones((chunk, GLA_HEAD_V), BF16)
    ng = ng_ref[...]

    for c in range(nchunks):
        sl = pl.ds(c * chunk, chunk)
        gk_hi, gk_lo = _split_bf16(gk_ref[sl, :])
        b = (jnp.dot(tril_bf, gk_hi, preferred_element_type=F32)
             + jnp.dot(tril_bf, gk_lo, preferred_element_type=F32))
        b_last_col = (lax.dot_general(gk_hi, ones, _TN, preferred_element_type=F32)
                      + lax.dot_general(gk_lo, ones, _TN, preferred_element_type=F32))
        b_last = b[chunk - 1:chunk, :]
        q = q_ref[sl, :].astype(F32) * (GLA_HEAD_K ** -0.5)
        k = k_ref[sl, :].astype(F32)
        v = v_ref[sl, :]
        q_in = (q * jnp.exp(b)).astype(BF16)
        k_in = (k * jnp.exp(-b)).astype(BF16)
        k_dec = (k * jnp.exp(b_last - b)).astype(BF16)
        a = lax.dot_general(q_in, k_in, _NT, preferred_element_type=F32)
        a = jnp.where(tril, a, 0.0)
        s_prev = s_scr[...]
        o = (jnp.dot(a.astype(BF16), v, preferred_element_type=F32)
             + jnp.dot(q_in, s_prev.astype(BF16), preferred_element_type=F32))
        s_scr[...] = jnp.exp(b_last_col) * s_prev + lax.dot_general(k_dec, v, _TN, preferred_element_type=F32)
        go = go_ref[sl, :].astype(F32)
        o_ref[sl, :] = (_rms(o, ng) * _silu(go)).astype(o_ref.dtype)

    @pl.when(blk == pl.num_programs(2) - 1)
    def _():
        sfin_ref[...] = s_scr[...]


def _gla(p, gk, norm_g, s0, layer, nbatch, seq, chunk, nchunks):
    blk_rows = chunk * nchunks
    nblk = seq // blk_rows
    qcol = COL_Q_G // GLA_HEAD_K
    kcol = COL_K_G // GLA_HEAD_K
    vcol = COL_V_G // GLA_HEAD_V
    gcol = COL_G_G // GLA_HEAD_V
    kernel = functools.partial(_gla_kernel, chunk=chunk, nchunks=nchunks)
    return pl.pallas_call(
        kernel,
        grid=(nbatch, GLA_HEADS, nblk),
        in_specs=[pl.BlockSpec((blk_rows, GLA_HEAD_K), lambda b, h, i: (b * nblk + i, qcol + h)),
                  pl.BlockSpec((blk_rows, GLA_HEAD_K), lambda b, h, i: (b * nblk + i, kcol + h)),
                  pl.BlockSpec((blk_rows, GLA_HEAD_V), lambda b, h, i: (b * nblk + i, vcol + h)),
                  pl.BlockSpec((blk_rows, GLA_HEAD_V), lambda b, h, i: (b * nblk + i, gcol + h)),
                  pl.BlockSpec((blk_rows, GLA_HEAD_K), lambda b, h, i: (b * nblk + i, h)),
                  pl.BlockSpec((None, 1, GLA_HEAD_V), lambda b, h, i: (layer, 0, 0)),
                  pl.BlockSpec((None, GLA_HEAD_K, GLA_HEAD_V), lambda b, h, i: (h, 0, 0))],
        out_specs=[pl.BlockSpec((blk_rows, GLA_HEAD_V), lambda b, h, i: (b * nblk + i, h)),
                   pl.BlockSpec((None, GLA_HEAD_K, GLA_HEAD_V), lambda b, h, i: (b * GLA_HEADS + h, 0, 0))],
        out_shape=[jax.ShapeDtypeStruct((nbatch * seq, GLA_WIDTH), BF16),
                   jax.ShapeDtypeStruct((nbatch * GLA_HEADS, GLA_HEAD_K, GLA_HEAD_V), F32)],
        scratch_shapes=[pltpu.VMEM((GLA_HEAD_K, GLA_HEAD_V), F32)],
        compiler_params=_params("parallel", "parallel", "arbitrary"),
        name="gla",
    )(p, p, p, p, gk, norm_g, s0)


def _deferred(step, nblocks, produce, finish):
    def emit(produce_parts, finish_parts):
        for i in range(max(len(produce_parts), len(finish_parts))):
            for parts in (produce_parts, finish_parts):
                if i < len(parts):
                    parts[i]()

    @pl.when(step == 0)
    def _():
        emit(produce(0), [])

    for slot in range(2):
        @pl.when(jnp.logical_and(jnp.logical_and(step > 0, step < nblocks), step % 2 == slot))
        def _(slot=slot):
            emit(produce(slot), finish(1 - slot))

    @pl.when(step == nblocks)
    def _():
        emit([], finish((nblocks - 1) % 2))


def _cur_block(s, nblocks):
    return jnp.minimum(s, nblocks - 1)


def _prev_block(s):
    return jnp.maximum(s - 1, 0)


_PROJ_PARTS = 4


def _residual_rows(rows, y_ref, h_ref, gpost_ref, gnext_ref, ho_ref, hn_ref):
    h = h_ref[rows, :] + _rms(y_ref[rows, :], gpost_ref[...])
    ho_ref[rows, :] = h
    hn_ref[rows, :] = _rms(h, gnext_ref[...]).astype(hn_ref.dtype)


def _proj_residual_kernel(*refs, nblocks, k_splits):
    nx = len(k_splits)
    x_refs, (w_ref, h_ref, gpost_ref, gnext_ref, ho_ref, hn_ref, y_scr) = refs[:nx], refs[nx:]
    bm, n = h_ref.shape
    nparts = _PROJ_PARTS if bm % (8 * _PROJ_PARTS) == 0 else 1

    def produce(slot):
        def part(c):
            cols = slice(c * (n // nparts), (c + 1) * (n // nparts))
            y, lo = None, 0
            for x_ref, k in zip(x_refs, k_splits):
                t = jnp.dot(x_ref[...], w_ref[lo:lo + k, cols], preferred_element_type=F32)
                y = t if y is None else y + t
                lo += k
            y_scr[slot, :, cols] = y
        return [functools.partial(part, c) for c in range(nparts)]

    def finish(slot):
        def part(r):
            rows = slice(r * (bm // nparts), (r + 1) * (bm // nparts))
            _residual_rows(rows, y_scr.at[slot], h_ref, gpost_ref, gnext_ref, ho_ref, hn_ref)
        return [functools.partial(part, r) for r in range(nparts)]

    _deferred(pl.program_id(0), nblocks, produce, finish)


def _proj_residual(xs, w, h, g_post, g_next, layer, bm, name):
    m = h.shape[0]
    nblocks = m // bm
    k_splits = tuple(x.shape[1] for x in xs)
    k_total = sum(k_splits)
    cur = lambda s: (_cur_block(s, nblocks), 0)
    prev = lambda s: (_prev_block(s), 0)
    const = lambda s: (0, 0)
    kernel = functools.partial(_proj_residual_kernel, nblocks=nblocks, k_splits=k_splits)
    return pl.pallas_call(
        kernel,
        grid=(nblocks + 1,),
        in_specs=[pl.BlockSpec((bm, k), cur) for k in k_splits] + [
            pl.BlockSpec((None, k_total, D_MODEL), lambda s: (layer, 0, 0), pipeline_mode=pl.Buffered(1)),
            pl.BlockSpec((bm, D_MODEL), prev),
            pl.BlockSpec((1, D_MODEL), const),
            pl.BlockSpec((1, D_MODEL), const)],
        out_specs=[pl.BlockSpec((bm, D_MODEL), prev), pl.BlockSpec((bm, D_MODEL), prev)],
        out_shape=[jax.ShapeDtypeStruct((m, D_MODEL), F32), jax.ShapeDtypeStruct((m, D_MODEL), BF16)],
        scratch_shapes=[pltpu.VMEM((2, bm, D_MODEL), F32)],
        compiler_params=_params("arbitrary"),
        name=name,
    )(*xs, w, h, g_post, g_next)


_HALO = 8
_UP_SUB_ROWS = 256


def _up_conv_kernel(x_ref, xm_ref, wa_ref, wv_ref, cwa_ref, cwv_ref, cba_ref, cbv_ref, o_ref, om_ref,
                    wbf_scr, ubuf, mbuf, *, bm, sub_rows, blocks_per_seq):
    i = pl.program_id(1)
    halves = ((wa_ref, cwa_ref, cba_ref), (wv_ref, cwv_ref, cbv_ref))
    meta_rows = slice(_HALO, _HALO + N_META)

    def conv3(buf, lo, rows, current, cw_ref, cb_ref):
        cw = cw_ref[...]
        return (cb_ref[...] + cw[0:1] * buf[lo - 2:lo - 2 + rows, :] + cw[1:2] * buf[lo - 1:lo - 1 + rows, :]
                + cw[2:3] * current)

    @pl.when(i == 0)
    def _():
        conv = []
        for idx, (w_ref, cw_ref, cb_ref) in enumerate(halves):
            wbf_scr[idx] = w_ref[...].astype(BF16)
            u = jnp.dot(xm_ref[...], wbf_scr[idx], preferred_element_type=F32)
            buf = mbuf.at[idx]
            buf[_HALO - 2:_HALO, :] = jnp.zeros((2, u.shape[1]), F32)
            buf[meta_rows, :] = u
            conv.append(conv3(buf, _HALO, N_META, u, cw_ref, cb_ref))
        om_ref[...] = (_silu(conv[0]) * conv[1]).astype(om_ref.dtype)

    first = (i % blocks_per_seq) == 0
    for idx in range(2):
        buf = ubuf.at[idx]

        @pl.when(first)
        def _():
            buf[_HALO - 2:_HALO, :] = mbuf[idx, _HALO + N_META - 2:_HALO + N_META, :]

        @pl.when(jnp.logical_not(first))
        def _():
            buf[_HALO - 2:_HALO, :] = buf[_HALO + bm - 2:_HALO + bm, :]

    sb = min(bm, sub_rows)
    for r in range(bm // sb):
        rows = slice(r * sb, (r + 1) * sb)
        lo = _HALO + r * sb
        x = x_ref[rows, :]
        conv = []
        for idx, (_, cw_ref, cb_ref) in enumerate(halves):
            u = jnp.dot(x, wbf_scr[idx], preferred_element_type=F32)
            buf = ubuf.at[idx]
            buf[lo:lo + sb, :] = u
            conv.append(conv3(buf, lo, sb, u, cw_ref, cb_ref))
        o_ref[rows, :] = (_silu(conv[0]) * conv[1]).astype(o_ref.dtype)


def _up_conv(hn, hn_meta, w_up, conv_w, conv_b, layer, bm, bn, blocks_per_seq):
    m = hn.shape[0]
    nj = D_FF // bn
    kernel = functools.partial(_up_conv_kernel, bm=bm, sub_rows=_UP_SUB_ROWS, blocks_per_seq=blocks_per_seq)
    return pl.pallas_call(
        kernel,
        grid=(nj, m // bm),
        in_specs=[pl.BlockSpec((bm, D_MODEL), lambda j, i: (i, 0)),
                  pl.BlockSpec((N_META, D_MODEL), lambda j, i: (0, 0)),
                  pl.BlockSpec((None, D_MODEL, bn), lambda j, i: (layer, 0, j)),
                  pl.BlockSpec((None, D_MODEL, bn), lambda j, i: (layer, 0, j + nj)),
                  pl.BlockSpec((None, 3, bn), lambda j, i: (layer, 0, j)),
                  pl.BlockSpec((None, 3, bn), lambda j, i: (layer, 0, j + nj)),
                  pl.BlockSpec((None, 1, bn), lambda j, i: (layer, 0, j)),
                  pl.BlockSpec((None, 1, bn), lambda j, i: (layer, 0, j + nj))],
        out_specs=[pl.BlockSpec((bm, bn), lambda j, i: (i, j)),
                   pl.BlockSpec((N_META, bn), lambda j, i: (0, j))],
        out_shape=[jax.ShapeDtypeStruct((m, D_FF), BF16),
                   jax.ShapeDtypeStruct((N_META, D_FF), BF16)],
        scratch_shapes=[pltpu.VMEM((2, D_MODEL, bn), BF16),
                        pltpu.VMEM((2, _HALO + bm, bn), F32),
                        pltpu.VMEM((2, _HALO + N_META, bn), F32)],
        compiler_params=_params("parallel", "arbitrary"),
        name="up_conv",
    )(hn, hn_meta, w_up, w_up, conv_w, conv_w, conv_b, conv_b)


def _lam_init(layer):
    return 0.8 - 0.6 * math.exp(-0.3 * layer)


_CFG = dict(bm_norm=512, bm_in=1024, bn_in=1024, bm_gate=1024, tq=512, da_heads=4, gla_chunks=32,
            bm_out=256, bm_up=1024, bn_up=512, bm_down=256)
_META_DA_HEADS = 2


def _trunk(h, hm, prm):
    cfg = _CFG
    hn = _entry_norm(h, prm["pre_mix_g"][0], cfg["bm_norm"])
    hnm = _entry_norm(hm, prm["pre_mix_g"][0], N_META)
    for l in range(DEPTH):
        lam_init = _lam_init(l)
        post_mix_g, pre_ffn_g, post_ffn_g = prm["post_mix_g"][l], prm["pre_ffn_g"][l], prm["post_ffn_g"][l]
        g_next = prm["pre_mix_g"][(l + 1) % DEPTH]
        p, pm = _in_proj(hn, hnm, prm["w_in"], l, cfg["bm_in"], cfg["bn_in"])
        gk = _gate(hn, prm["w_lr"], prm["w2"], prm["b2"], l, cfg["bm_gate"])
        gkm = _gate(hnm, prm["w_lr"], prm["w2"], prm["b2"], l, N_META)
        om_da = _diff_attention(pm, pm, prm["da_lambda"], prm["da_subln_g"], l, lam_init,
                                1, N_META, N_META, _META_DA_HEADS, has_prefix=False)
        s_zero = jnp.zeros((GLA_HEADS, GLA_HEAD_K, GLA_HEAD_V), F32)
        om_gla, s_meta = _gla(pm, gkm, prm["gla_norm_g"], s_zero, l, 1, N_META, N_META, 1)
        o_da = _diff_attention(p, pm, prm["da_lambda"], prm["da_subln_g"], l, lam_init,
                               BATCH, SEQ, cfg["tq"], cfg["da_heads"], has_prefix=True)
        o_gla, _ = _gla(p, gk, prm["gla_norm_g"], s_meta, l, BATCH, SEQ, GLA_CHUNK, cfg["gla_chunks"])
        h, hn = _proj_residual((o_da, o_gla), prm["w_out"], h, post_mix_g, pre_ffn_g, l, cfg["bm_out"], "out_proj")
        hm, hnm = _proj_residual((om_da, om_gla), prm["w_out"], hm, post_mix_g, pre_ffn_g, l, N_META, "out_proj")
        act, actm = _up_conv(hn, hnm, prm["w_up"], prm["conv_w"], prm["conv_b"], l,
                             cfg["bm_up"], cfg["bn_up"], SEQ // cfg["bm_up"])
        h, hn = _proj_residual((act,), prm["w_down"], h, post_ffn_g, g_next, l, cfg["bm_down"], "down_proj")
        if l + 1 < DEPTH:
            hm, hnm = _proj_residual((actm,), prm["w_down"], hm, post_ffn_g, g_next, l, N_META, "down_proj")
    return h


def kernel(x, meta_tokens, pre_mix_g, w_in, da_lambda, da_subln_g, gla_gate_w2, gla_gate_b, gla_norm_g, w_out,
           post_mix_g, pre_ffn_g, w_up, conv_w, conv_b, w_down, post_ffn_g):
    vec = lambda a: a.astype(F32).reshape(DEPTH, 1, a.shape[-1])
    prm = dict(
        pre_mix_g=vec(pre_mix_g), post_mix_g=vec(post_mix_g), pre_ffn_g=vec(pre_ffn_g), post_ffn_g=vec(post_ffn_g),
        w_in=w_in.astype(F32), w_up=w_up.astype(F32),
        w_lr=jnp.pad(w_in[:, :, IN_MAIN:], ((0, 0), (0, 0), (0, LANES - GLA_GATE_RANK))).astype(BF16),
        w2=jnp.pad(gla_gate_w2, ((0, 0), (0, LANES - GLA_GATE_RANK), (0, 0))).astype(BF16),
        b2=vec(gla_gate_b),
        da_lambda=da_lambda.astype(F32), da_subln_g=vec(da_subln_g), gla_norm_g=vec(gla_norm_g),
        w_out=w_out.astype(BF16), w_down=w_down.astype(BF16),
        conv_w=conv_w.astype(F32), conv_b=vec(conv_b),
    )
    h = _trunk(x.astype(F32).reshape(BATCH * SEQ, D_MODEL), meta_tokens.astype(F32), prm)
    return h.reshape(BATCH, SEQ, D_MODEL).astype(x.dtype)
```

```python
import functools
import math

import jax
import jax.numpy as jnp
from jax import lax
from jax.experimental import pallas as pl
from jax.experimental.pallas import tpu as pltpu

D_MODEL = 2048
BATCH = 4
SEQ = 2048
DEPTH = 4
N_META = 16

DA_HEADS = 8
DA_HEAD_DIM = 64
DA_HEAD_WIDTH = 2 * DA_HEAD_DIM
DA_WIDTH = DA_HEADS * DA_HEAD_WIDTH
GLA_HEADS = 4
GLA_WIDTH = 1024
GLA_HEAD_V = 256
GLA_KEY_WIDTH = 512
GLA_HEAD_K = 128
GLA_GATE_RANK = 16
GLA_GATE_NORM = 16.0
GLA_CHUNK = 64
D_FF = 5632
EPS = 1e-6

COL_Q_DA = 0
COL_K_DA = DA_WIDTH
COL_V_DA = 2 * DA_WIDTH
COL_Q_G = 3 * DA_WIDTH
COL_K_G = COL_Q_G + GLA_KEY_WIDTH
COL_V_G = COL_K_G + GLA_KEY_WIDTH
COL_G_G = COL_V_G + GLA_WIDTH
COL_LR = COL_G_G + GLA_WIDTH
IN_MAIN = COL_LR
LANES = 128

VMEM_LIMIT = 56 * 1024 * 1024

F32 = jnp.float32
BF16 = jnp.bfloat16

_NT = (((1,), (1,)), ((), ()))
_TN = (((0,), (0,)), ((), ()))
_LOG2E = math.log2(math.e)


def _params(*sem):
    return pltpu.CompilerParams(dimension_semantics=sem, vmem_limit_bytes=VMEM_LIMIT)


def _rms(x, g):
    return x * lax.rsqrt(jnp.mean(x * x, axis=-1, keepdims=True) + EPS) * g


def _silu(x):
    return x * (1.0 / (1.0 + jnp.exp(-x)))


def _norm_kernel(x_ref, g_ref, o_ref):
    o_ref[...] = _rms(x_ref[...], g_ref[...]).astype(o_ref.dtype)


def _entry_norm(x, g, bm):
    m = x.shape[0]
    return pl.pallas_call(
        _norm_kernel,
        grid=(m // bm,),
        in_specs=[pl.BlockSpec((bm, D_MODEL), lambda i: (i, 0)),
                  pl.BlockSpec((1, D_MODEL), lambda i: (0, 0))],
        out_specs=pl.BlockSpec((bm, D_MODEL), lambda i: (i, 0)),
        out_shape=jax.ShapeDtypeStruct((m, D_MODEL), BF16),
        compiler_params=_params("parallel"),
        name="entry_norm",
    )(x, g)


def _in_proj_kernel(x_ref, xm_ref, w_ref, o_ref, om_ref):
    @pl.when(pl.program_id(1) == 0)
    def _():
        om_ref[...] = jnp.dot(xm_ref[...], w_ref[...], preferred_element_type=F32).astype(om_ref.dtype)

    o_ref[...] = jnp.dot(x_ref[...], w_ref[...], preferred_element_type=F32).astype(o_ref.dtype)


def _in_proj(hn, hn_meta, w_in, layer, bm, bn):
    m = hn.shape[0]
    return pl.pallas_call(
        _in_proj_kernel,
        grid=(IN_MAIN // bn, m // bm),
        in_specs=[pl.BlockSpec((bm, D_MODEL), lambda j, i: (i, 0)),
                  pl.BlockSpec((N_META, D_MODEL), lambda j, i: (0, 0)),
                  pl.BlockSpec((None, D_MODEL, bn), lambda j, i: (layer, 0, j))],
        out_specs=[pl.BlockSpec((bm, bn), lambda j, i: (i, j)),
                   pl.BlockSpec((N_META, bn), lambda j, i: (0, j))],
        out_shape=[jax.ShapeDtypeStruct((m, IN_MAIN), BF16),
                   jax.ShapeDtypeStruct((N_META, IN_MAIN), BF16)],
        compiler_params=_params("parallel", "arbitrary"),
        name="in_proj",
    )(hn, hn_meta, w_in)


def _gate_kernel(x_ref, wlr_ref, w2_ref, b2_ref, o_ref):
    lr = jnp.dot(x_ref[...], wlr_ref[...], preferred_element_type=F32)
    z = jnp.dot(lr.astype(BF16), w2_ref[...], preferred_element_type=F32) + b2_ref[...]
    log_sig = jnp.minimum(z, 0.0) - jnp.log1p(jnp.exp(-jnp.abs(z)))
    o_ref[...] = log_sig * (1.0 / GLA_GATE_NORM)


def _gate(hn, w_lr, w2, b2, layer, bm):
    m = hn.shape[0]
    return pl.pallas_call(
        _gate_kernel,
        grid=(m // bm,),
        in_specs=[pl.BlockSpec((bm, D_MODEL), lambda i: (i, 0)),
                  pl.BlockSpec((None, D_MODEL, LANES), lambda i: (layer, 0, 0)),
                  pl.BlockSpec((None, LANES, GLA_KEY_WIDTH), lambda i: (layer, 0, 0)),
                  pl.BlockSpec((None, 1, GLA_KEY_WIDTH), lambda i: (layer, 0, 0))],
        out_specs=pl.BlockSpec((bm, GLA_KEY_WIDTH), lambda i: (i, 0)),
        out_shape=jax.ShapeDtypeStruct((m, GLA_KEY_WIDTH), F32),
        compiler_params=_params("parallel"),
        name="gla_gate",
    )(hn, w_lr, w2, b2)


def _da_kernel(lam_ref, g_ref, q_ref, k_ref, v_ref, kp_ref, vp_ref, o_ref, vx_scr, vpx_scr,
               *, tq, heads, lam_init, has_prefix):
    qi = pl.program_id(2)
    seq = v_ref.shape[0]
    hw = DA_HEAD_WIDTH
    head_lanes = [slice(g * hw, (g + 1) * hw) for g in range(heads)]

    @pl.when(qi == 0)
    def _():
        for g, lanes in enumerate(head_lanes):
            vx_scr[g, :, :hw] = v_ref[:, lanes]
            vx_scr[g, :, hw:] = jnp.ones((seq, hw), BF16)
            if has_prefix:
                vpx_scr[g, :, :hw] = vp_ref[:, lanes]
                vpx_scr[g, :, hw:] = jnp.ones((N_META, hw), BF16)

    lane = lax.broadcasted_iota(jnp.int32, (1, hw), 1)
    causal = (lax.broadcasted_iota(jnp.int32, (tq, tq), 0) >= lax.broadcasted_iota(jnp.int32, (tq, tq), 1))
    lamv = lam_ref[...]
    lam = (jnp.exp(jnp.sum(lamv[0:1] * lamv[1:2], axis=-1, keepdims=True))
           - jnp.exp(jnp.sum(lamv[2:3] * lamv[3:4], axis=-1, keepdims=True)) + lam_init)

    def attend(n_below):
        kv_len = (n_below + 1) * tq
        for g, lanes in enumerate(head_lanes):
            q = q_ref[:, lanes].astype(F32) * (DA_HEAD_DIM ** -0.5 * _LOG2E)
            comps = []
            for qc in (jnp.where(lane < DA_HEAD_DIM, q, 0.0).astype(BF16),
                       jnp.where(lane >= DA_HEAD_DIM, q, 0.0).astype(BF16)):
                s = lax.dot_general(qc, k_ref[:kv_len, lanes], _NT, preferred_element_type=F32)
                parts = [jnp.where(causal, s[:, n_below * tq:], -jnp.inf)]
                if n_below:
                    parts.insert(0, s[:, :n_below * tq])
                m = functools.reduce(jnp.maximum, [jnp.max(part, axis=-1, keepdims=True) for part in parts])
                if has_prefix:
                    sp = lax.dot_general(qc, kp_ref[:, lanes], _NT, preferred_element_type=F32)
                    m = jnp.maximum(m, jnp.max(sp, axis=-1, keepdims=True))
                p = [jnp.exp2(part - m).astype(BF16) for part in parts]
                p = p[0] if len(p) == 1 else jnp.concatenate(p, axis=1)
                acc = jnp.dot(p, vx_scr[g, :kv_len, :], preferred_element_type=F32)
                if has_prefix:
                    acc = acc + jnp.dot(jnp.exp2(sp - m).astype(BF16), vpx_scr[g], preferred_element_type=F32)
                comps.append(acc[:, :hw] / acc[:, hw:])
            o = comps[0] - lam * comps[1]
            o_ref[:, lanes] = (_rms(o, g_ref[...]) * (1.0 - lam_init)).astype(o_ref.dtype)

    for n in range(seq // tq):
        pl.when(qi == n)(functools.partial(attend, n))


def _diff_attention(p, p_prefix, lam, subln_g, layer, lam_init, nbatch, seq, tq, heads, has_prefix):
    nq = seq // tq
    width = heads * DA_HEAD_WIDTH
    kcol = COL_K_DA // width
    vcol = COL_V_DA // width
    kernel = functools.partial(_da_kernel, tq=tq, heads=heads, lam_init=lam_init, has_prefix=has_prefix)
    return pl.pallas_call(
        kernel,
        grid=(nbatch, DA_HEADS // heads, nq),
        in_specs=[pl.BlockSpec((None, 4, DA_HEAD_DIM), lambda b, h, i: (layer, 0, 0)),
                  pl.BlockSpec((None, 1, DA_HEAD_WIDTH), lambda b, h, i: (layer, 0, 0)),
                  pl.BlockSpec((tq, width), lambda b, h, i: (b * nq + i, h)),
                  pl.BlockSpec((seq, width), lambda b, h, i: (b, kcol + h)),
                  pl.BlockSpec((seq, width), lambda b, h, i: (b, vcol + h)),
                  pl.BlockSpec((N_META, width), lambda b, h, i: (0, kcol + h)),
                  pl.BlockSpec((N_META, width), lambda b, h, i: (0, vcol + h))],
        out_specs=pl.BlockSpec((tq, width), lambda b, h, i: (b * nq + i, h)),
        out_shape=jax.ShapeDtypeStruct((nbatch * seq, DA_WIDTH), BF16),
        scratch_shapes=[pltpu.VMEM((heads, seq, 2 * DA_HEAD_WIDTH), BF16),
                        pltpu.VMEM((heads, N_META, 2 * DA_HEAD_WIDTH), BF16)],
        compiler_params=_params("parallel", "parallel", "arbitrary"),
        name="diff_attention",
    )(lam, subln_g, p, p, p, p_prefix, p_prefix)


def _split_bf16(x):
    hi = x.astype(BF16)
    lo = (x - hi.astype(F32)).astype(BF16)
    return hi, lo


def _gla_kernel(q_ref, k_ref, v_ref, go_ref, gk_ref, ng_ref, s0_ref, o_ref, sfin_ref, s_scr,
                *, chunk, nchunks):
    blk = pl.program_id(2)

    @pl.when(blk == 0)
    def _():
        s_scr[...] = s0_ref[...]

    row = lax.broadcasted_iota(jnp.int32, (chunk, chunk), 0)
    col = lax.broadcasted_iota(jnp.int32, (chunk, chunk), 1)
    tril = row >= col
    tril_bf = tril.astype(F32).astype(BF16)
    ones = jnp.ones((chunk, GLA_HEAD_V), BF16)
    ng = ng_ref[...]

    for c in range(nchunks):
        sl = pl.ds(c * chunk, chunk)
        gk_hi, gk_lo = _split_bf16(gk_ref[sl, :])
        b = (jnp.dot(tril_bf, gk_hi, preferred_element_type=F32)
             + jnp.dot(tril_bf, gk_lo, preferred_element_type=F32))
        b_last_col = (lax.dot_general(gk_hi, ones, _TN, preferred_element_type=F32)
                      + lax.dot_general(gk_lo, ones, _TN, preferred_element_type=F32))
        b_last = b[chunk - 1:chunk, :]
        q = q_ref[sl, :].astype(F32) * (GLA_HEAD_K ** -0.5)
        k = k_ref[sl, :].astype(F32)
        v = v_ref[sl, :]
        q_in = (q * jnp.exp(b)).astype(BF16)
        k_in = (k * jnp.exp(-b)).astype(BF16)
        k_dec = (k * jnp.exp(b_last - b)).astype(BF16)
        a = lax.dot_general(q_in, k_in, _NT, preferred_element_type=F32)
        a = jnp.where(tril, a, 0.0)
        s_prev = s_scr[...]
        o = (jnp.dot(a.astype(BF16), v, preferred_element_type=F32)
             + jnp.dot(q_in, s_prev.astype(BF16), preferred_element_type=F32))
        s_scr[...] = jnp.exp(b_last_col) * s_prev + lax.dot_general(k_dec, v, _TN, preferred_element_type=F32)
        go = go_ref[sl, :].astype(F32)
        o_ref[sl, :] = (_rms(o, ng) * _silu(go)).astype(o_ref.dtype)

    @pl.when(blk == pl.num_programs(2) - 1)
    def _():
        sfin_ref[...] = s_scr[...]


def _gla(p, gk, norm_g, s0, layer, nbatch, seq, chunk, nchunks):
    blk_rows = chunk * nchunks
    nblk = seq // blk_rows
    qcol = COL_Q_G // GLA_HEAD_K
    kcol = COL_K_G // GLA_HEAD_K
    vcol = COL_V_G // GLA_HEAD_V
    gcol = COL_G_G // GLA_HEAD_V
    kernel = functools.partial(_gla_kernel, chunk=chunk, nchunks=nchunks)
    return pl.pallas_call(
        kernel,
        grid=(nbatch, GLA_HEADS, nblk),
        in_specs=[pl.BlockSpec((blk_rows, GLA_HEAD_K), lambda b, h, i: (b * nblk + i, qcol + h)),
                  pl.BlockSpec((blk_rows, GLA_HEAD_K), lambda b, h, i: (b * nblk + i, kcol + h)),
                  pl.BlockSpec((blk_rows, GLA_HEAD_V), lambda b, h, i: (b * nblk + i, vcol + h)),
                  pl.BlockSpec((blk_rows, GLA_HEAD_V), lambda b, h, i: (b * nblk + i, gcol + h)),
                  pl.BlockSpec((blk_rows, GLA_HEAD_K), lambda b, h, i: (b * nblk + i, h)),
                  pl.BlockSpec((None, 1, GLA_HEAD_V), lambda b, h, i: (layer, 0, 0)),
                  pl.BlockSpec((None, GLA_HEAD_K, GLA_HEAD_V), lambda b, h, i: (h, 0, 0))],
        out_specs=[pl.BlockSpec((blk_rows, GLA_HEAD_V), lambda b, h, i: (b * nblk + i, h)),
                   pl.BlockSpec((None, GLA_HEAD_K, GLA_HEAD_V), lambda b, h, i: (b * GLA_HEADS + h, 0, 0))],
        out_shape=[jax.ShapeDtypeStruct((nbatch * seq, GLA_WIDTH), BF16),
                   jax.ShapeDtypeStruct((nbatch * GLA_HEADS, GLA_HEAD_K, GLA_HEAD_V), F32)],
        scratch_shapes=[pltpu.VMEM((GLA_HEAD_K, GLA_HEAD_V), F32)],
        compiler_params=_params("parallel", "parallel", "arbitrary"),
        name="gla",
    )(p, p, p, p, gk, norm_g, s0)


def _deferred(step, nblocks, produce, finish):
    def emit(produce_parts, finish_parts):
        for i in range(max(len(produce_parts), len(finish_parts))):
            for parts in (produce_parts, finish_parts):
                if i < len(parts):
                    parts[i]()

    @pl.when(step == 0)
    def _():
        emit(produce(0), [])

    for slot in range(2):
        @pl.when(jnp.logical_and(jnp.logical_and(step > 0, step < nblocks), step % 2 == slot))
        def _(slot=slot):
            emit(produce(slot), finish(1 - slot))

    @pl.when(step == nblocks)
    def _():
        emit([], finish((nblocks - 1) % 2))


def _cur_block(s, nblocks):
    return jnp.minimum(s, nblocks - 1)


def _prev_block(s):
    return jnp.maximum(s - 1, 0)


_PROJ_PARTS = 4


def _residual_rows(rows, y_ref, h_ref, gpost_ref, gnext_ref, ho_ref, hn_ref):
    h = h_ref[rows, :] + _rms(y_ref[rows, :], gpost_ref[...])
    ho_ref[rows, :] = h
    hn_ref[rows, :] = _rms(h, gnext_ref[...]).astype(hn_ref.dtype)


def _proj_residual_kernel(*refs, nblocks, k_splits):
    nx = len(k_splits)
    x_refs, (w_ref, h_ref, gpost_ref, gnext_ref, ho_ref, hn_ref, y_scr) = refs[:nx], refs[nx:]
    bm, n = h_ref.shape
    nparts = _PROJ_PARTS if bm % (8 * _PROJ_PARTS) == 0 else 1

    def produce(slot):
        def part(c):
            cols = slice(c * (n // nparts), (c + 1) * (n // nparts))
            y, lo = None, 0
            for x_ref, k in zip(x_refs, k_splits):
                t = jnp.dot(x_ref[...], w_ref[lo:lo + k, cols], preferred_element_type=F32)
                y = t if y is None else y + t
                lo += k
            y_scr[slot, :, cols] = y
        return [functools.partial(part, c) for c in range(nparts)]

    def finish(slot):
        def part(r):
            rows = slice(r * (bm // nparts), (r + 1) * (bm // nparts))
            _residual_rows(rows, y_scr.at[slot], h_ref, gpost_ref, gnext_ref, ho_ref, hn_ref)
        return [functools.partial(part, r) for r in range(nparts)]

    _deferred(pl.program_id(0), nblocks, produce, finish)


def _proj_residual(xs, w, h, g_post, g_next, layer, bm, name):
    m = h.shape[0]
    nblocks = m // bm
    k_splits = tuple(x.shape[1] for x in xs)
    k_total = sum(k_splits)
    cur = lambda s: (_cur_block(s, nblocks), 0)
    prev = lambda s: (_prev_block(s), 0)
    const = lambda s: (0, 0)
    kernel = functools.partial(_proj_residual_kernel, nblocks=nblocks, k_splits=k_splits)
    return pl.pallas_call(
        kernel,
        grid=(nblocks + 1,),
        in_specs=[pl.BlockSpec((bm, k), cur) for k in k_splits] + [
            pl.BlockSpec((None, k_total, D_MODEL), lambda s: (layer, 0, 0), pipeline_mode=pl.Buffered(1)),
            pl.BlockSpec((bm, D_MODEL), prev),
            pl.BlockSpec((1, D_MODEL), const),
            pl.BlockSpec((1, D_MODEL), const)],
        out_specs=[pl.BlockSpec((bm, D_MODEL), prev), pl.BlockSpec((bm, D_MODEL), prev)],
        out_shape=[jax.ShapeDtypeStruct((m, D_MODEL), F32), jax.ShapeDtypeStruct((m, D_MODEL), BF16)],
        scratch_shapes=[pltpu.VMEM((2, bm, D_MODEL), F32)],
        compiler_params=_params("arbitrary"),
        name=name,
    )(*xs, w, h, g_post, g_next)


_HALO = 8
_UP_SUB_ROWS = 256


def _up_conv_kernel(x_ref, xm_ref, wa_ref, wv_ref, cwa_ref, cwv_ref, cba_ref, cbv_ref, o_ref, om_ref,
                    wbf_scr, ubuf, mbuf, *, bm, sub_rows, blocks_per_seq):
    i = pl.program_id(1)
    bn = o_ref.shape[1]
    half_lanes = (slice(0, bn), slice(bn, 2 * bn))
    conv_params = ((cwa_ref, cba_ref), (cwv_ref, cbv_ref))

    def gated(buf, lo, rows, u):
        conv = []
        for lanes, (cw_ref, cb_ref) in zip(half_lanes, conv_params):
            cw = cw_ref[...]
            conv.append(cb_ref[...] + cw[0:1] * buf[lo - 2:lo - 2 + rows, lanes]
                        + cw[1:2] * buf[lo - 1:lo - 1 + rows, lanes] + cw[2:3] * u[:, lanes])
        return _silu(conv[0]) * conv[1]

    @pl.when(i == 0)
    def _():
        wbf_scr[:, half_lanes[0]] = wa_ref[...].astype(BF16)
        wbf_scr[:, half_lanes[1]] = wv_ref[...].astype(BF16)
        u = jnp.dot(xm_ref[...], wbf_scr[...], preferred_element_type=F32)
        mbuf[_HALO - 2:_HALO, :] = jnp.zeros((2, 2 * bn), F32)
        mbuf[_HALO:_HALO + N_META, :] = u
        om_ref[...] = gated(mbuf, _HALO, N_META, u).astype(om_ref.dtype)

    first = (i % blocks_per_seq) == 0

    @pl.when(first)
    def _():
        ubuf[_HALO - 2:_HALO, :] = mbuf[_HALO + N_META - 2:_HALO + N_META, :]

    @pl.when(jnp.logical_not(first))
    def _():
        ubuf[_HALO - 2:_HALO, :] = ubuf[_HALO + bm - 2:_HALO + bm, :]

    sb = min(bm, sub_rows)
    for r in range(bm // sb):
        rows = slice(r * sb, (r + 1) * sb)
        lo = _HALO + r * sb
        u = jnp.dot(x_ref[rows, :], wbf_scr[...], preferred_element_type=F32)
        ubuf[lo:lo + sb, :] = u
        o_ref[rows, :] = gated(ubuf, lo, sb, u).astype(o_ref.dtype)


def _up_conv(hn, hn_meta, w_up, conv_w, conv_b, layer, bm, bn, blocks_per_seq):
    m = hn.shape[0]
    nj = D_FF // bn
    kernel = functools.partial(_up_conv_kernel, bm=bm, sub_rows=_UP_SUB_ROWS, blocks_per_seq=blocks_per_seq)
    return pl.pallas_call(
        kernel,
        grid=(nj, m // bm),
        in_specs=[pl.BlockSpec((bm, D_MODEL), lambda j, i: (i, 0)),
                  pl.BlockSpec((N_META, D_MODEL), lambda j, i: (0, 0)),
                  pl.BlockSpec((None, D_MODEL, bn), lambda j, i: (layer, 0, j)),
                  pl.BlockSpec((None, D_MODEL, bn), lambda j, i: (layer, 0, j + nj)),
                  pl.BlockSpec((None, 3, bn), lambda j, i: (layer, 0, j)),
                  pl.BlockSpec((None, 3, bn), lambda j, i: (layer, 0, j + nj)),
                  pl.BlockSpec((None, 1, bn), lambda j, i: (layer, 0, j)),
                  pl.BlockSpec((None, 1, bn), lambda j, i: (layer, 0, j + nj))],
        out_specs=[pl.BlockSpec((bm, bn), lambda j, i: (i, j)),
                   pl.BlockSpec((N_META, bn), lambda j, i: (0, j))],
        out_shape=[jax.ShapeDtypeStruct((m, D_FF), BF16),
                   jax.ShapeDtypeStruct((N_META, D_FF), BF16)],
        scratch_shapes=[pltpu.VMEM((D_MODEL, 2 * bn), BF16),
                        pltpu.VMEM((_HALO + bm, 2 * bn), F32),
                        pltpu.VMEM((_HALO + N_META, 2 * bn), F32)],
        compiler_params=_params("parallel", "arbitrary"),
        name="up_conv",
    )(hn, hn_meta, w_up, w_up, conv_w, conv_w, conv_b, conv_b)


def _lam_init(layer):
    return 0.8 - 0.6 * math.exp(-0.3 * layer)


_CFG = dict(bm_norm=512, bm_in=1024, bn_in=1024, bm_gate=1024, tq=512, da_heads=4, gla_chunks=32,
            bm_out=256, bm_up=1024, bn_up=512, bm_down=256)
_META_DA_HEADS = 2


def _trunk(h, hm, prm):
    cfg = _CFG
    hn = _entry_norm(h, prm["pre_mix_g"][0], cfg["bm_norm"])
    hnm = _entry_norm(hm, prm["pre_mix_g"][0], N_META)
    for l in range(DEPTH):
        lam_init = _lam_init(l)
        post_mix_g, pre_ffn_g, post_ffn_g = prm["post_mix_g"][l], prm["pre_ffn_g"][l], prm["post_ffn_g"][l]
        g_next = prm["pre_mix_g"][(l + 1) % DEPTH]
        p, pm = _in_proj(hn, hnm, prm["w_in"], l, cfg["bm_in"], cfg["bn_in"])
        gk = _gate(hn, prm["w_lr"], prm["w2"], prm["b2"], l, cfg["bm_gate"])
        gkm = _gate(hnm, prm["w_lr"], prm["w2"], prm["b2"], l, N_META)
        om_da = _diff_attention(pm, pm, prm["da_lambda"], prm["da_subln_g"], l, lam_init,
                                1, N_META, N_META, _META_DA_HEADS, has_prefix=False)
        s_zero = jnp.zeros((GLA_HEADS, GLA_HEAD_K, GLA_HEAD_V), F32)
        om_gla, s_meta = _gla(pm, gkm, prm["gla_norm_g"], s_zero, l, 1, N_META, N_META, 1)
        o_da = _diff_attention(p, pm, prm["da_lambda"], prm["da_subln_g"], l, lam_init,
                               BATCH, SEQ, cfg["tq"], cfg["da_heads"], has_prefix=True)
        o_gla, _ = _gla(p, gk, prm["gla_norm_g"], s_meta, l, BATCH, SEQ, GLA_CHUNK, cfg["gla_chunks"])
        h, hn = _proj_residual((o_da, o_gla), prm["w_out"], h, post_mix_g, pre_ffn_g, l, cfg["bm_out"], "out_proj")
        hm, hnm = _proj_residual((om_da, om_gla), prm["w_out"], hm, post_mix_g, pre_ffn_g, l, N_META, "out_proj")
        act, actm = _up_conv(hn, hnm, prm["w_up"], prm["conv_w"], prm["conv_b"], l,
                             cfg["bm_up"], cfg["bn_up"], SEQ // cfg["bm_up"])
        h, hn = _proj_residual((act,), prm["w_down"], h, post_ffn_g, g_next, l, cfg["bm_down"], "down_proj")
        if l + 1 < DEPTH:
            hm, hnm = _proj_residual((actm,), prm["w_down"], hm, post_ffn_g, g_next, l, N_META, "down_proj")
    return h


def kernel(x, meta_tokens, pre_mix_g, w_in, da_lambda, da_subln_g, gla_gate_w2, gla_gate_b, gla_norm_g, w_out,
           post_mix_g, pre_ffn_g, w_up, conv_w, conv_b, w_down, post_ffn_g):
    vec = lambda a: a.astype(F32).reshape(DEPTH, 1, a.shape[-1])
    prm = dict(
        pre_mix_g=vec(pre_mix_g), post_mix_g=vec(post_mix_g), pre_ffn_g=vec(pre_ffn_g), post_ffn_g=vec(post_ffn_g),
        w_in=w_in.astype(BF16),
        w_up=w_up.astype(F32),
        w_lr=jnp.pad(w_in[:, :, IN_MAIN:], ((0, 0), (0, 0), (0, LANES - GLA_GATE_RANK))).astype(BF16),
        w2=jnp.pad(gla_gate_w2, ((0, 0), (0, LANES - GLA_GATE_RANK), (0, 0))).astype(BF16),
        b2=vec(gla_gate_b),
        da_lambda=da_lambda.astype(F32), da_subln_g=vec(da_subln_g), gla_norm_g=vec(gla_norm_g),
        w_out=w_out.astype(BF16), w_down=w_down.astype(BF16),
        conv_w=conv_w.astype(F32), conv_b=vec(conv_b),
    )
    h = _trunk(x.astype(F32).reshape(BATCH * SEQ, D_MODEL), meta_tokens.astype(F32), prm)
    return h.reshape(BATCH, SEQ, D_MODEL).astype(x.dtype)
```

```python
import functools
import math

import jax
import jax.numpy as jnp
from jax import lax
from jax.experimental import pallas as pl
from jax.experimental.pallas import tpu as pltpu

D_MODEL = 2048
BATCH = 4
SEQ = 2048
DEPTH = 4
N_META = 16

DA_HEADS = 8
DA_HEAD_DIM = 64
DA_HEAD_WIDTH = 2 * DA_HEAD_DIM
DA_WIDTH = DA_HEADS * DA_HEAD_WIDTH
GLA_HEADS = 4
GLA_WIDTH = 1024
GLA_HEAD_V = 256
GLA_KEY_WIDTH = 512
GLA_HEAD_K = 128
GLA_GATE_RANK = 16
GLA_GATE_NORM = 16.0
GLA_CHUNK = 64
D_FF = 5632
EPS = 1e-6

COL_Q_DA = 0
COL_K_DA = DA_WIDTH
COL_V_DA = 2 * DA_WIDTH
COL_Q_G = 3 * DA_WIDTH
COL_K_G = COL_Q_G + GLA_KEY_WIDTH
COL_V_G = COL_K_G + GLA_KEY_WIDTH
COL_G_G = COL_V_G + GLA_WIDTH
COL_LR = COL_G_G + GLA_WIDTH
IN_MAIN = COL_LR
LANES = 128

VMEM_LIMIT = 56 * 1024 * 1024

F32 = jnp.float32
BF16 = jnp.bfloat16

_NT = (((1,), (1,)), ((), ()))
_TN = (((0,), (0,)), ((), ()))
_LOG2E = math.log2(math.e)


def _params(*sem):
    return pltpu.CompilerParams(dimension_semantics=sem, vmem_limit_bytes=VMEM_LIMIT)


def _rms(x, g):
    return x * lax.rsqrt(jnp.mean(x * x, axis=-1, keepdims=True) + EPS) * g


def _silu(x):
    return x * (1.0 / (1.0 + jnp.exp(-x)))


def _norm_kernel(x_ref, g_ref, o_ref):
    o_ref[...] = _rms(x_ref[...], g_ref[...]).astype(o_ref.dtype)


def _entry_norm(x, g, bm):
    m = x.shape[0]
    return pl.pallas_call(
        _norm_kernel,
        grid=(m // bm,),
        in_specs=[pl.BlockSpec((bm, D_MODEL), lambda i: (i, 0)),
                  pl.BlockSpec((1, D_MODEL), lambda i: (0, 0))],
        out_specs=pl.BlockSpec((bm, D_MODEL), lambda i: (i, 0)),
        out_shape=jax.ShapeDtypeStruct((m, D_MODEL), BF16),
        compiler_params=_params("parallel"),
        name="entry_norm",
    )(x, g)


def _in_proj_kernel(x_ref, xm_ref, w_ref, o_ref, om_ref):
    @pl.when(pl.program_id(1) == 0)
    def _():
        om_ref[...] = jnp.dot(xm_ref[...], w_ref[...], preferred_element_type=F32).astype(om_ref.dtype)

    o_ref[...] = jnp.dot(x_ref[...], w_ref[...], preferred_element_type=F32).astype(o_ref.dtype)


def _in_proj(hn, hn_meta, w_in, layer, bm, bn):
    m = hn.shape[0]
    return pl.pallas_call(
        _in_proj_kernel,
        grid=(IN_MAIN // bn, m // bm),
        in_specs=[pl.BlockSpec((bm, D_MODEL), lambda j, i: (i, 0)),
                  pl.BlockSpec((N_META, D_MODEL), lambda j, i: (0, 0)),
                  pl.BlockSpec((None, D_MODEL, bn), lambda j, i: (layer, 0, j))],
        out_specs=[pl.BlockSpec((bm, bn), lambda j, i: (i, j)),
                   pl.BlockSpec((N_META, bn), lambda j, i: (0, j))],
        out_shape=[jax.ShapeDtypeStruct((m, IN_MAIN), BF16),
                   jax.ShapeDtypeStruct((N_META, IN_MAIN), BF16)],
        compiler_params=_params("parallel", "arbitrary"),
        name="in_proj",
    )(hn, hn_meta, w_in)


def _gate_kernel(x_ref, wlr_ref, w2_ref, b2_ref, o_ref):
    lr = jnp.dot(x_ref[...], wlr_ref[...], preferred_element_type=F32)
    z = jnp.dot(lr.astype(BF16), w2_ref[...], preferred_element_type=F32) + b2_ref[...]
    log_sig = jnp.minimum(z, 0.0) - jnp.log1p(jnp.exp(-jnp.abs(z)))
    o_ref[...] = log_sig * (1.0 / GLA_GATE_NORM)


def _gate(hn, w_lr, w2, b2, layer, bm):
    m = hn.shape[0]
    return pl.pallas_call(
        _gate_kernel,
        grid=(m // bm,),
        in_specs=[pl.BlockSpec((bm, D_MODEL), lambda i: (i, 0)),
                  pl.BlockSpec((None, D_MODEL, LANES), lambda i: (layer, 0, 0)),
                  pl.BlockSpec((None, LANES, GLA_KEY_WIDTH), lambda i: (layer, 0, 0)),
                  pl.BlockSpec((None, 1, GLA_KEY_WIDTH), lambda i: (layer, 0, 0))],
        out_specs=pl.BlockSpec((bm, GLA_KEY_WIDTH), lambda i: (i, 0)),
        out_shape=jax.ShapeDtypeStruct((m, GLA_KEY_WIDTH), F32),
        compiler_params=_params("parallel"),
        name="gla_gate",
    )(hn, w_lr, w2, b2)


def _da_kernel(lam_ref, g_ref, q_ref, k_ref, v_ref, kp_ref, vp_ref, o_ref, vx_scr, vpx_scr,
               *, tq, heads, lam_init, has_prefix):
    qi = pl.program_id(2)
    seq = v_ref.shape[0]
    hw = DA_HEAD_WIDTH
    head_lanes = [slice(g * hw, (g + 1) * hw) for g in range(heads)]

    @pl.when(qi == 0)
    def _():
        for g, lanes in enumerate(head_lanes):
            vx_scr[g, :, :hw] = v_ref[:, lanes]
            vx_scr[g, :, hw:] = jnp.ones((seq, hw), BF16)
            if has_prefix:
                vpx_scr[g, :, :hw] = vp_ref[:, lanes]
                vpx_scr[g, :, hw:] = jnp.ones((N_META, hw), BF16)

    lane = lax.broadcasted_iota(jnp.int32, (1, hw), 1)
    causal = (lax.broadcasted_iota(jnp.int32, (tq, tq), 0) >= lax.broadcasted_iota(jnp.int32, (tq, tq), 1))
    lamv = lam_ref[...]
    lam = (jnp.exp(jnp.sum(lamv[0:1] * lamv[1:2], axis=-1, keepdims=True))
           - jnp.exp(jnp.sum(lamv[2:3] * lamv[3:4], axis=-1, keepdims=True)) + lam_init)

    def attend(n_below):
        kv_len = (n_below + 1) * tq
        for g, lanes in enumerate(head_lanes):
            q = q_ref[:, lanes].astype(F32) * (DA_HEAD_DIM ** -0.5 * _LOG2E)
            comps = []
            for qc in (jnp.where(lane < DA_HEAD_DIM, q, 0.0).astype(BF16),
                       jnp.where(lane >= DA_HEAD_DIM, q, 0.0).astype(BF16)):
                s = lax.dot_general(qc, k_ref[:kv_len, lanes], _NT, preferred_element_type=F32)
                parts = [jnp.where(causal, s[:, n_below * tq:], -jnp.inf)]
                if n_below:
                    parts.insert(0, s[:, :n_below * tq])
                m = functools.reduce(jnp.maximum, [jnp.max(part, axis=-1, keepdims=True) for part in parts])
                if has_prefix:
                    sp = lax.dot_general(qc, kp_ref[:, lanes], _NT, preferred_element_type=F32)
                    m = jnp.maximum(m, jnp.max(sp, axis=-1, keepdims=True))
                p = [jnp.exp2(part - m).astype(BF16) for part in parts]
                p = p[0] if len(p) == 1 else jnp.concatenate(p, axis=1)
                acc = jnp.dot(p, vx_scr[g, :kv_len, :], preferred_element_type=F32)
                if has_prefix:
                    acc = acc + jnp.dot(jnp.exp2(sp - m).astype(BF16), vpx_scr[g], preferred_element_type=F32)
                comps.append(acc[:, :hw] / acc[:, hw:])
            o = comps[0] - lam * comps[1]
            o_ref[:, lanes] = (_rms(o, g_ref[...]) * (1.0 - lam_init)).astype(o_ref.dtype)

    for n in range(seq // tq):
        pl.when(qi == n)(functools.partial(attend, n))


def _diff_attention(p, p_prefix, lam, subln_g, layer, lam_init, nbatch, seq, tq, heads, has_prefix):
    nq = seq // tq
    width = heads * DA_HEAD_WIDTH
    kcol = COL_K_DA // width
    vcol = COL_V_DA // width
    kernel = functools.partial(_da_kernel, tq=tq, heads=heads, lam_init=lam_init, has_prefix=has_prefix)
    return pl.pallas_call(
        kernel,
        grid=(nbatch, DA_HEADS // heads, nq),
        in_specs=[pl.BlockSpec((None, 4, DA_HEAD_DIM), lambda b, h, i: (layer, 0, 0)),
                  pl.BlockSpec((None, 1, DA_HEAD_WIDTH), lambda b, h, i: (layer, 0, 0)),
                  pl.BlockSpec((tq, width), lambda b, h, i: (b * nq + i, h)),
                  pl.BlockSpec((seq, width), lambda b, h, i: (b, kcol + h)),
                  pl.BlockSpec((seq, width), lambda b, h, i: (b, vcol + h)),
                  pl.BlockSpec((N_META, width), lambda b, h, i: (0, kcol + h)),
                  pl.BlockSpec((N_META, width), lambda b, h, i: (0, vcol + h))],
        out_specs=pl.BlockSpec((tq, width), lambda b, h, i: (b * nq + i, h)),
        out_shape=jax.ShapeDtypeStruct((nbatch * seq, DA_WIDTH), BF16),
        scratch_shapes=[pltpu.VMEM((heads, seq, 2 * DA_HEAD_WIDTH), BF16),
                        pltpu.VMEM((heads, N_META, 2 * DA_HEAD_WIDTH), BF16)],
        compiler_params=_params("parallel", "parallel", "arbitrary"),
        name="diff_attention",
    )(lam, subln_g, p, p, p, p_prefix, p_prefix)


def _split_bf16(x):
    hi = x.astype(BF16)
    lo = (x - hi.astype(F32)).astype(BF16)
    return hi, lo


def _gla_kernel(q_ref, k_ref, v_ref, go_ref, gk_ref, ng_ref, s0_ref, o_ref, sfin_ref, s_scr,
                *, chunk, nchunks):
    blk = pl.program_id(2)

    @pl.when(blk == 0)
    def _():
        s_scr[...] = s0_ref[...]

    row = lax.broadcasted_iota(jnp.int32, (chunk, chunk), 0)
    col = lax.broadcasted_iota(jnp.int32, (chunk, chunk), 1)
    tril = row >= col
    tril_bf = tril.astype(F32).astype(BF16)
    ones = jnp.ones((chunk, GLA_HEAD_V), BF16)
    ng = ng_ref[...]

    for c in range(nchunks):
        sl = pl.ds(c * chunk, chunk)
        gk_hi, gk_lo = _split_bf16(gk_ref[sl, :])
        b = (jnp.dot(tril_bf, gk_hi, preferred_element_type=F32)
             + jnp.dot(tril_bf, gk_lo, preferred_element_type=F32))
        b_last_col = (lax.dot_general(gk_hi, ones, _TN, preferred_element_type=F32)
                      + lax.dot_general(gk_lo, ones, _TN, preferred_element_type=F32))
        b_last = b[chunk - 1:chunk, :]
        q = q_ref[sl, :].astype(F32) * (GLA_HEAD_K ** -0.5)
        k = k_ref[sl, :].astype(F32)
        v = v_ref[sl, :]
        q_in = (q * jnp.exp(b)).astype(BF16)
        k_in = (k * jnp.exp(-b)).astype(BF16)
        k_dec = (k * jnp.exp(b_last - b)).astype(BF16)
        a = lax.dot_general(q_in, k_in, _NT, preferred_element_type=F32)
        a = jnp.where(tril, a, 0.0)
        s_prev = s_scr[...]
        o = (jnp.dot(a.astype(BF16), v, preferred_element_type=F32)
             + jnp.dot(q_in, s_prev.astype(BF16), preferred_element_type=F32))
        s_scr[...] = jnp.exp(b_last_col) * s_prev + lax.dot_general(k_dec, v, _TN, preferred_element_type=F32)
        go = go_ref[sl, :].astype(F32)
        o_ref[sl, :] = (_rms(o, ng) * _silu(go)).astype(o_ref.dtype)

    @pl.when(blk == pl.num_programs(2) - 1)
    def _():
        sfin_ref[...] = s_scr[...]


def _gla(p, gk, norm_g, s0, layer, nbatch, seq, chunk, nchunks):
    blk_rows = chunk * nchunks
    nblk = seq // blk_rows
    qcol = COL_Q_G // GLA_HEAD_K
    kcol = COL_K_G // GLA_HEAD_K
    vcol = COL_V_G // GLA_HEAD_V
    gcol = COL_G_G // GLA_HEAD_V
    kernel = functools.partial(_gla_kernel, chunk=chunk, nchunks=nchunks)
    return pl.pallas_call(
        kernel,
        grid=(nbatch, GLA_HEADS, nblk),
        in_specs=[pl.BlockSpec((blk_rows, GLA_HEAD_K), lambda b, h, i: (b * nblk + i, qcol + h)),
                  pl.BlockSpec((blk_rows, GLA_HEAD_K), lambda b, h, i: (b * nblk + i, kcol + h)),
                  pl.BlockSpec((blk_rows, GLA_HEAD_V), lambda b, h, i: (b * nblk + i, vcol + h)),
                  pl.BlockSpec((blk_rows, GLA_HEAD_V), lambda b, h, i: (b * nblk + i, gcol + h)),
                  pl.BlockSpec((blk_rows, GLA_HEAD_K), lambda b, h, i: (b * nblk + i, h)),
                  pl.BlockSpec((None, 1, GLA_HEAD_V), lambda b, h, i: (layer, 0, 0)),
                  pl.BlockSpec((None, GLA_HEAD_K, GLA_HEAD_V), lambda b, h, i: (h, 0, 0))],
        out_specs=[pl.BlockSpec((blk_rows, GLA_HEAD_V), lambda b, h, i: (b * nblk + i, h)),
                   pl.BlockSpec((None, GLA_HEAD_K, GLA_HEAD_V), lambda b, h, i: (b * GLA_HEADS + h, 0, 0))],
        out_shape=[jax.ShapeDtypeStruct((nbatch * seq, GLA_WIDTH), BF16),
                   jax.ShapeDtypeStruct((nbatch * GLA_HEADS, GLA_HEAD_K, GLA_HEAD_V), F32)],
        scratch_shapes=[pltpu.VMEM((GLA_HEAD_K, GLA_HEAD_V), F32)],
        compiler_params=_params("parallel", "parallel", "arbitrary"),
        name="gla",
    )(p, p, p, p, gk, norm_g, s0)


def _deferred(step, nblocks, produce, finish):
    def emit(produce_parts, finish_parts):
        for i in range(max(len(produce_parts), len(finish_parts))):
            for parts in (produce_parts, finish_parts):
                if i < len(parts):
                    parts[i]()

    @pl.when(step == 0)
    def _():
        emit(produce(0), [])

    for slot in range(2):
        @pl.when(jnp.logical_and(jnp.logical_and(step > 0, step < nblocks), step % 2 == slot))
        def _(slot=slot):
            emit(produce(slot), finish(1 - slot))

    @pl.when(step == nblocks)
    def _():
        emit([], finish((nblocks - 1) % 2))


def _cur_block(s, nblocks):
    return jnp.minimum(s, nblocks - 1)


def _prev_block(s):
    return jnp.maximum(s - 1, 0)


_PROJ_PARTS = 4


def _residual_rows(rows, y_ref, h_ref, gpost_ref, gnext_ref, ho_ref, hn_ref):
    h = h_ref[rows, :] + _rms(y_ref[rows, :], gpost_ref[...])
    ho_ref[rows, :] = h
    hn_ref[rows, :] = _rms(h, gnext_ref[...]).astype(hn_ref.dtype)


def _proj_residual_kernel(*refs, nblocks, k_splits):
    nx = len(k_splits)
    x_refs, (w_ref, h_ref, gpost_ref, gnext_ref, ho_ref, hn_ref, y_scr) = refs[:nx], refs[nx:]
    bm, n = h_ref.shape
    nparts = _PROJ_PARTS if bm % (8 * _PROJ_PARTS) == 0 else 1

    def produce(slot):
        def part(c):
            cols = slice(c * (n // nparts), (c + 1) * (n // nparts))
            y, lo = None, 0
            for x_ref, k in zip(x_refs, k_splits):
                t = jnp.dot(x_ref[...], w_ref[lo:lo + k, cols], preferred_element_type=F32)
                y = t if y is None else y + t
                lo += k
            y_scr[slot, :, cols] = y
        return [functools.partial(part, c) for c in range(nparts)]

    def finish(slot):
        def part(r):
            rows = slice(r * (bm // nparts), (r + 1) * (bm // nparts))
            _residual_rows(rows, y_scr.at[slot], h_ref, gpost_ref, gnext_ref, ho_ref, hn_ref)
        return [functools.partial(part, r) for r in range(nparts)]

    _deferred(pl.program_id(0), nblocks, produce, finish)


def _proj_residual(xs, w, h, g_post, g_next, layer, bm, name):
    m = h.shape[0]
    nblocks = m // bm
    k_splits = tuple(x.shape[1] for x in xs)
    k_total = sum(k_splits)
    cur = lambda s: (_cur_block(s, nblocks), 0)
    prev = lambda s: (_prev_block(s), 0)
    const = lambda s: (0, 0)
    kernel = functools.partial(_proj_residual_kernel, nblocks=nblocks, k_splits=k_splits)
    return pl.pallas_call(
        kernel,
        grid=(nblocks + 1,),
        in_specs=[pl.BlockSpec((bm, k), cur) for k in k_splits] + [
            pl.BlockSpec((None, k_total, D_MODEL), lambda s: (layer, 0, 0), pipeline_mode=pl.Buffered(1)),
            pl.BlockSpec((bm, D_MODEL), prev),
            pl.BlockSpec((1, D_MODEL), const),
            pl.BlockSpec((1, D_MODEL), const)],
        out_specs=[pl.BlockSpec((bm, D_MODEL), prev), pl.BlockSpec((bm, D_MODEL), prev)],
        out_shape=[jax.ShapeDtypeStruct((m, D_MODEL), F32), jax.ShapeDtypeStruct((m, D_MODEL), BF16)],
        scratch_shapes=[pltpu.VMEM((2, bm, D_MODEL), F32)],
        compiler_params=_params("arbitrary"),
        name=name,
    )(*xs, w, h, g_post, g_next)


_HALO = 8
_UP_SUB_ROWS = 256


def _up_conv_kernel(x_ref, xm_ref, wa_ref, wv_ref, cwa_ref, cwv_ref, cba_ref, cbv_ref, o_ref, om_ref,
                    wbf_scr, ubuf, mbuf, *, bm, sub_rows, blocks_per_seq):
    i = pl.program_id(1)
    bn = o_ref.shape[1]
    half_lanes = (slice(0, bn), slice(bn, 2 * bn))
    conv_params = ((cwa_ref, cba_ref), (cwv_ref, cbv_ref))

    def gated(x, buf, lo, rows):
        conv = []
        for lanes, (cw_ref, cb_ref) in zip(half_lanes, conv_params):
            u = jnp.dot(x, wbf_scr[:, lanes], preferred_element_type=F32)
            buf[lo:lo + rows, lanes] = u
            cw = cw_ref[...]
            conv.append(cb_ref[...] + cw[0:1] * buf[lo - 2:lo - 2 + rows, lanes]
                        + cw[1:2] * buf[lo - 1:lo - 1 + rows, lanes] + cw[2:3] * u)
        return _silu(conv[0]) * conv[1]

    @pl.when(i == 0)
    def _():
        wbf_scr[:, half_lanes[0]] = wa_ref[...].astype(BF16)
        wbf_scr[:, half_lanes[1]] = wv_ref[...].astype(BF16)
        mbuf[_HALO - 2:_HALO, :] = jnp.zeros((2, 2 * bn), F32)
        om_ref[...] = gated(xm_ref[...], mbuf, _HALO, N_META).astype(om_ref.dtype)

    first = (i % blocks_per_seq) == 0

    @pl.when(first)
    def _():
        ubuf[_HALO - 2:_HALO, :] = mbuf[_HALO + N_META - 2:_HALO + N_META, :]

    @pl.when(jnp.logical_not(first))
    def _():
        ubuf[_HALO - 2:_HALO, :] = ubuf[_HALO + bm - 2:_HALO + bm, :]

    sb = min(bm, sub_rows)
    for r in range(bm // sb):
        rows = slice(r * sb, (r + 1) * sb)
        o_ref[rows, :] = gated(x_ref[rows, :], ubuf, _HALO + r * sb, sb).astype(o_ref.dtype)


def _up_conv(hn, hn_meta, w_up, conv_w, conv_b, layer, bm, bn, blocks_per_seq):
    m = hn.shape[0]
    nj = D_FF // bn
    kernel = functools.partial(_up_conv_kernel, bm=bm, sub_rows=_UP_SUB_ROWS, blocks_per_seq=blocks_per_seq)
    return pl.pallas_call(
        kernel,
        grid=(nj, m // bm),
        in_specs=[pl.BlockSpec((bm, D_MODEL), lambda j, i: (i, 0)),
                  pl.BlockSpec((N_META, D_MODEL), lambda j, i: (0, 0)),
                  pl.BlockSpec((None, D_MODEL, bn), lambda j, i: (layer, 0, j)),
                  pl.BlockSpec((None, D_MODEL, bn), lambda j, i: (layer, 0, j + nj)),
                  pl.BlockSpec((None, 3, bn), lambda j, i: (layer, 0, j)),
                  pl.BlockSpec((None, 3, bn), lambda j, i: (layer, 0, j + nj)),
                  pl.BlockSpec((None, 1, bn), lambda j, i: (layer, 0, j)),
                  pl.BlockSpec((None, 1, bn), lambda j, i: (layer, 0, j + nj))],
        out_specs=[pl.BlockSpec((bm, bn), lambda j, i: (i, j)),
                   pl.BlockSpec((N_META, bn), lambda j, i: (0, j))],
        out_shape=[jax.ShapeDtypeStruct((m, D_FF), BF16),
                   jax.ShapeDtypeStruct((N_META, D_FF), BF16)],
        scratch_shapes=[pltpu.VMEM((D_MODEL, 2 * bn), BF16),
                        pltpu.VMEM((_HALO + bm, 2 * bn), F32),
                        pltpu.VMEM((_HALO + N_META, 2 * bn), F32)],
        compiler_params=_params("parallel", "arbitrary"),
        name="up_conv",
    )(hn, hn_meta, w_up, w_up, conv_w, conv_w, conv_b, conv_b)


def _lam_init(layer):
    return 0.8 - 0.6 * math.exp(-0.3 * layer)


_CFG = dict(bm_norm=512, bm_in=1024, bn_in=1024, bm_gate=1024, tq=512, da_heads=4, gla_chunks=32,
            bm_out=256, bm_up=1024, bn_up=512, bm_down=256)
_META_DA_HEADS = 2


def _trunk(h, hm, prm):
    cfg = _CFG
    hn = _entry_norm(h, prm["pre_mix_g"][0], cfg["bm_norm"])
    hnm = _entry_norm(hm, prm["pre_mix_g"][0], N_META)
    for l in range(DEPTH):
        lam_init = _lam_init(l)
        post_mix_g, pre_ffn_g, post_ffn_g = prm["post_mix_g"][l], prm["pre_ffn_g"][l], prm["post_ffn_g"][l]
        g_next = prm["pre_mix_g"][(l + 1) % DEPTH]
        p, pm = _in_proj(hn, hnm, prm["w_in"], l, cfg["bm_in"], cfg["bn_in"])
        gk = _gate(hn, prm["w_lr"], prm["w2"], prm["b2"], l, cfg["bm_gate"])
        gkm = _gate(hnm, prm["w_lr"], prm["w2"], prm["b2"], l, N_META)
        om_da = _diff_attention(pm, pm, prm["da_lambda"], prm["da_subln_g"], l, lam_init,
                                1, N_META, N_META, _META_DA_HEADS, has_prefix=False)
        s_zero = jnp.zeros((GLA_HEADS, GLA_HEAD_K, GLA_HEAD_V), F32)
        om_gla, s_meta = _gla(pm, gkm, prm["gla_norm_g"], s_zero, l, 1, N_META, N_META, 1)
        o_da = _diff_attention(p, pm, prm["da_lambda"], prm["da_subln_g"], l, lam_init,
                               BATCH, SEQ, cfg["tq"], cfg["da_heads"], has_prefix=True)
        o_gla, _ = _gla(p, gk, prm["gla_norm_g"], s_meta, l, BATCH, SEQ, GLA_CHUNK, cfg["gla_chunks"])
        h, hn = _proj_residual((o_da, o_gla), prm["w_out"], h, post_mix_g, pre_ffn_g, l, cfg["bm_out"], "out_proj")
        hm, hnm = _proj_residual((om_da, om_gla), prm["w_out"], hm, post_mix_g, pre_ffn_g, l, N_META, "out_proj")
        act, actm = _up_conv(hn, hnm, prm["w_up"], prm["conv_w"], prm["conv_b"], l,
                             cfg["bm_up"], cfg["bn_up"], SEQ // cfg["bm_up"])
        h, hn = _proj_residual((act,), prm["w_down"], h, post_ffn_g, g_next, l, cfg["bm_down"], "down_proj")
        if l + 1 < DEPTH:
            hm, hnm = _proj_residual((actm,), prm["w_down"], hm, post_ffn_g, g_next, l, N_META, "down_proj")
    return h


def kernel(x, meta_tokens, pre_mix_g, w_in, da_lambda, da_subln_g, gla_gate_w2, gla_gate_b, gla_norm_g, w_out,
           post_mix_g, pre_ffn_g, w_up, conv_w, conv_b, w_down, post_ffn_g):
    vec = lambda a: a.astype(F32).reshape(DEPTH, 1, a.shape[-1])
    prm = dict(
        pre_mix_g=vec(pre_mix_g), post_mix_g=vec(post_mix_g), pre_ffn_g=vec(pre_ffn_g), post_ffn_g=vec(post_ffn_g),
        w_in=w_in.astype(BF16),
        w_up=w_up.astype(F32),
        w_lr=jnp.pad(w_in[:, :, IN_MAIN:], ((0, 0), (0, 0), (0, LANES - GLA_GATE_RANK))).astype(BF16),
        w2=jnp.pad(gla_gate_w2, ((0, 0), (0, LANES - GLA_GATE_RANK), (0, 0))).astype(BF16),
        b2=vec(gla_gate_b),
        da_lambda=da_lambda.astype(F32), da_subln_g=vec(da_subln_g), gla_norm_g=vec(gla_norm_g),
        w_out=w_out.astype(BF16), w_down=w_down.astype(BF16),
        conv_w=conv_w.astype(F32), conv_b=vec(conv_b),
    )
    h = _trunk(x.astype(F32).reshape(BATCH * SEQ, D_MODEL), meta_tokens.astype(F32), prm)
    return h.reshape(BATCH, SEQ, D_MODEL).astype(x.dtype)
```

```python
import functools
import math

import jax
import jax.numpy as jnp
from jax import lax
from jax.experimental import pallas as pl
from jax.experimental.pallas import tpu as pltpu

D_MODEL = 2048
BATCH = 4
SEQ = 2048
DEPTH = 4
N_META = 16

DA_HEADS = 8
DA_HEAD_DIM = 64
DA_HEAD_WIDTH = 2 * DA_HEAD_DIM
DA_WIDTH = DA_HEADS * DA_HEAD_WIDTH
GLA_HEADS = 4
GLA_WIDTH = 1024
GLA_HEAD_V = 256
GLA_KEY_WIDTH = 512
GLA_HEAD_K = 128
GLA_GATE_RANK = 16
GLA_GATE_NORM = 16.0
GLA_CHUNK = 64
D_FF = 5632
EPS = 1e-6

COL_Q_DA = 0
COL_K_DA = DA_WIDTH
COL_V_DA = 2 * DA_WIDTH
COL_Q_G = 3 * DA_WIDTH
COL_K_G = COL_Q_G + GLA_KEY_WIDTH
COL_V_G = COL_K_G + GLA_KEY_WIDTH
COL_G_G = COL_V_G + GLA_WIDTH
COL_LR = COL_G_G + GLA_WIDTH
IN_MAIN = COL_LR
LANES = 128

VMEM_LIMIT = 56 * 1024 * 1024

F32 = jnp.float32
BF16 = jnp.bfloat16

_NT = (((1,), (1,)), ((), ()))
_TN = (((0,), (0,)), ((), ()))
_LOG2E = math.log2(math.e)


def _params(*sem):
    return pltpu.CompilerParams(dimension_semantics=sem, vmem_limit_bytes=VMEM_LIMIT)


def _rms(x, g):
    return x * lax.rsqrt(jnp.mean(x * x, axis=-1, keepdims=True) + EPS) * g


def _silu(x):
    return x * (1.0 / (1.0 + jnp.exp(-x)))


def _norm_kernel(x_ref, g_ref, o_ref):
    o_ref[...] = _rms(x_ref[...], g_ref[...]).astype(o_ref.dtype)


def _entry_norm(x, g, bm):
    m = x.shape[0]
    return pl.pallas_call(
        _norm_kernel,
        grid=(m // bm,),
        in_specs=[pl.BlockSpec((bm, D_MODEL), lambda i: (i, 0)),
                  pl.BlockSpec((1, D_MODEL), lambda i: (0, 0))],
        out_specs=pl.BlockSpec((bm, D_MODEL), lambda i: (i, 0)),
        out_shape=jax.ShapeDtypeStruct((m, D_MODEL), BF16),
        compiler_params=_params("parallel"),
        name="entry_norm",
    )(x, g)


def _in_proj_kernel(x_ref, xm_ref, w_ref, o_ref, om_ref):
    @pl.when(pl.program_id(1) == 0)
    def _():
        om_ref[...] = jnp.dot(xm_ref[...], w_ref[...], preferred_element_type=F32).astype(om_ref.dtype)

    o_ref[...] = jnp.dot(x_ref[...], w_ref[...], preferred_element_type=F32).astype(o_ref.dtype)


def _in_proj(hn, hn_meta, w_in, layer, bm, bn):
    m = hn.shape[0]
    return pl.pallas_call(
        _in_proj_kernel,
        grid=(IN_MAIN // bn, m // bm),
        in_specs=[pl.BlockSpec((bm, D_MODEL), lambda j, i: (i, 0)),
                  pl.BlockSpec((N_META, D_MODEL), lambda j, i: (0, 0)),
                  pl.BlockSpec((None, D_MODEL, bn), lambda j, i: (layer, 0, j))],
        out_specs=[pl.BlockSpec((bm, bn), lambda j, i: (i, j)),
                   pl.BlockSpec((N_META, bn), lambda j, i: (0, j))],
        out_shape=[jax.ShapeDtypeStruct((m, IN_MAIN), BF16),
                   jax.ShapeDtypeStruct((N_META, IN_MAIN), BF16)],
        compiler_params=_params("parallel", "arbitrary"),
        name="in_proj",
    )(hn, hn_meta, w_in)


def _gate_kernel(x_ref, wlr_ref, w2_ref, b2_ref, o_ref):
    lr = jnp.dot(x_ref[...], wlr_ref[...], preferred_element_type=F32)
    z = jnp.dot(lr.astype(BF16), w2_ref[...], preferred_element_type=F32) + b2_ref[...]
    log_sig = jnp.minimum(z, 0.0) - jnp.log1p(jnp.exp(-jnp.abs(z)))
    o_ref[...] = log_sig * (1.0 / GLA_GATE_NORM)


def _gate(hn, w_lr, w2, b2, layer, bm):
    m = hn.shape[0]
    return pl.pallas_call(
        _gate_kernel,
        grid=(m // bm,),
        in_specs=[pl.BlockSpec((bm, D_MODEL), lambda i: (i, 0)),
                  pl.BlockSpec((None, D_MODEL, LANES), lambda i: (layer, 0, 0)),
                  pl.BlockSpec((None, LANES, GLA_KEY_WIDTH), lambda i: (layer, 0, 0)),
                  pl.BlockSpec((None, 1, GLA_KEY_WIDTH), lambda i: (layer, 0, 0))],
        out_specs=pl.BlockSpec((bm, GLA_KEY_WIDTH), lambda i: (i, 0)),
        out_shape=jax.ShapeDtypeStruct((m, GLA_KEY_WIDTH), F32),
        compiler_params=_params("parallel"),
        name="gla_gate",
    )(hn, w_lr, w2, b2)


def _da_kernel(lam_ref, g_ref, q_ref, k_ref, v_ref, kp_ref, vp_ref, o_ref, vx_scr, vpx_scr,
               *, tq, heads, lam_init, has_prefix):
    qi = pl.program_id(2)
    seq = v_ref.shape[0]
    hw = DA_HEAD_WIDTH
    head_lanes = [slice(g * hw, (g + 1) * hw) for g in range(heads)]

    @pl.when(qi == 0)
    def _():
        for g, lanes in enumerate(head_lanes):
            vx_scr[g, :, :hw] = v_ref[:, lanes]
            vx_scr[g, :, hw:] = jnp.ones((seq, hw), BF16)
            if has_prefix:
                vpx_scr[g, :, :hw] = vp_ref[:, lanes]
                vpx_scr[g, :, hw:] = jnp.ones((N_META, hw), BF16)

    lane = lax.broadcasted_iota(jnp.int32, (1, hw), 1)
    causal = (lax.broadcasted_iota(jnp.int32, (tq, tq), 0) >= lax.broadcasted_iota(jnp.int32, (tq, tq), 1))
    lamv = lam_ref[...]
    lam = (jnp.exp(jnp.sum(lamv[0:1] * lamv[1:2], axis=-1, keepdims=True))
           - jnp.exp(jnp.sum(lamv[2:3] * lamv[3:4], axis=-1, keepdims=True)) + lam_init)

    def attend(n_below):
        kv_len = (n_below + 1) * tq
        for g, lanes in enumerate(head_lanes):
            q = q_ref[:, lanes].astype(F32) * (DA_HEAD_DIM ** -0.5 * _LOG2E)
            comps = []
            for qc in (jnp.where(lane < DA_HEAD_DIM, q, 0.0).astype(BF16),
                       jnp.where(lane >= DA_HEAD_DIM, q, 0.0).astype(BF16)):
                s = lax.dot_general(qc, k_ref[:kv_len, lanes], _NT, preferred_element_type=F32)
                parts = [jnp.where(causal, s[:, n_below * tq:], -jnp.inf)]
                if n_below:
                    parts.insert(0, s[:, :n_below * tq])
                m = functools.reduce(jnp.maximum, [jnp.max(part, axis=-1, keepdims=True) for part in parts])
                if has_prefix:
                    sp = lax.dot_general(qc, kp_ref[:, lanes], _NT, preferred_element_type=F32)
                    m = jnp.maximum(m, jnp.max(sp, axis=-1, keepdims=True))
                p = [jnp.exp2(part - m).astype(BF16) for part in parts]
                p = p[0] if len(p) == 1 else jnp.concatenate(p, axis=1)
                acc = jnp.dot(p, vx_scr[g, :kv_len, :], preferred_element_type=F32)
                if has_prefix:
                    acc = acc + jnp.dot(jnp.exp2(sp - m).astype(BF16), vpx_scr[g], preferred_element_type=F32)
                comps.append(acc[:, :hw] / acc[:, hw:])
            o = comps[0] - lam * comps[1]
            o_ref[:, lanes] = (_rms(o, g_ref[...]) * (1.0 - lam_init)).astype(o_ref.dtype)

    for n in range(seq // tq):
        pl.when(qi == n)(functools.partial(attend, n))


def _diff_attention(p, p_prefix, lam, subln_g, layer, lam_init, nbatch, seq, tq, heads, has_prefix):
    nq = seq // tq
    width = heads * DA_HEAD_WIDTH
    kcol = COL_K_DA // width
    vcol = COL_V_DA // width
    kernel = functools.partial(_da_kernel, tq=tq, heads=heads, lam_init=lam_init, has_prefix=has_prefix)
    return pl.pallas_call(
        kernel,
        grid=(nbatch, DA_HEADS // heads, nq),
        in_specs=[pl.BlockSpec((None, 4, DA_HEAD_DIM), lambda b, h, i: (layer, 0, 0)),
                  pl.BlockSpec((None, 1, DA_HEAD_WIDTH), lambda b, h, i: (layer, 0, 0)),
                  pl.BlockSpec((tq, width), lambda b, h, i: (b * nq + i, h)),
                  pl.BlockSpec((seq, width), lambda b, h, i: (b, kcol + h)),
                  pl.BlockSpec((seq, width), lambda b, h, i: (b, vcol + h)),
                  pl.BlockSpec((N_META, width), lambda b, h, i: (0, kcol + h)),
                  pl.BlockSpec((N_META, width), lambda b, h, i: (0, vcol + h))],
        out_specs=pl.BlockSpec((tq, width), lambda b, h, i: (b * nq + i, h)),
        out_shape=jax.ShapeDtypeStruct((nbatch * seq, DA_WIDTH), BF16),
        scratch_shapes=[pltpu.VMEM((heads, seq, 2 * DA_HEAD_WIDTH), BF16),
                        pltpu.VMEM((heads, N_META, 2 * DA_HEAD_WIDTH), BF16)],
        compiler_params=_params("parallel", "parallel", "arbitrary"),
        name="diff_attention",
    )(lam, subln_g, p, p, p, p_prefix, p_prefix)


def _split_bf16(x):
    hi = x.astype(BF16)
    lo = (x - hi.astype(F32)).astype(BF16)
    return hi, lo


def _gla_kernel(q_ref, k_ref, v_ref, go_ref, gk_ref, ng_ref, s0_ref, o_ref, sfin_ref, s_scr,
                *, chunk, nchunks):
    blk = pl.program_id(2)

    @pl.when(blk == 0)
    def _():
        s_scr[...] = s0_ref[...]

    row = lax.broadcasted_iota(jnp.int32, (chunk, chunk), 0)
    col = lax.broadcasted_iota(jnp.int32, (chunk, chunk), 1)
    tril = row >= col
    tril_bf = tril.astype(F32).astype(BF16)
    ones = jnp.ones((chunk, GLA_HEAD_V), BF16)
    ng = ng_ref[...]

    for c in range(nchunks):
        sl = pl.ds(c * chunk, chunk)
        gk_hi, gk_lo = _split_bf16(gk_ref[sl, :])
        b = (jnp.dot(tril_bf, gk_hi, preferred_element_type=F32)
             + jnp.dot(tril_bf, gk_lo, preferred_element_type=F32))
        b_last_col = (lax.dot_general(gk_hi, ones, _TN, preferred_element_type=F32)
                      + lax.dot_general(gk_lo, ones, _TN, preferred_element_type=F32))
        b_last = b[chunk - 1:chunk, :]
        q = q_ref[sl, :].astype(F32) * (GLA_HEAD_K ** -0.5)
        k = k_ref[sl, :].astype(F32)
        v = v_ref[sl, :]
        q_in = (q * jnp.exp(b)).astype(BF16)
        k_in = (k * jnp.exp(-b)).astype(BF16)
        k_dec = (k * jnp.exp(b_last - b)).astype(BF16)
        a = lax.dot_general(q_in, k_in, _NT, preferred_element_type=F32)
        a = jnp.where(tril, a, 0.0)
        s_prev = s_scr[...]
        o = (jnp.dot(a.astype(BF16), v, preferred_element_type=F32)
             + jnp.dot(q_in, s_prev.astype(BF16), preferred_element_type=F32))
        s_scr[...] = jnp.exp(b_last_col) * s_prev + lax.dot_general(k_dec, v, _TN, preferred_element_type=F32)
        go = go_ref[sl, :].astype(F32)
        o_ref[sl, :] = (_rms(o, ng) * _silu(go)).astype(o_ref.dtype)

    @pl.when(blk == pl.num_programs(2) - 1)
    def _():
        sfin_ref[...] = s_scr[...]


def _gla(p, gk, norm_g, s0, layer, nbatch, seq, chunk, nchunks):
    blk_rows = chunk * nchunks
    nblk = seq // blk_rows
    qcol = COL_Q_G // GLA_HEAD_K
    kcol = COL_K_G // GLA_HEAD_K
    vcol = COL_V_G // GLA_HEAD_V
    gcol = COL_G_G // GLA_HEAD_V
    kernel = functools.partial(_gla_kernel, chunk=chunk, nchunks=nchunks)
    return pl.pallas_call(
        kernel,
        grid=(nbatch, GLA_HEADS, nblk),
        in_specs=[pl.BlockSpec((blk_rows, GLA_HEAD_K), lambda b, h, i: (b * nblk + i, qcol + h)),
                  pl.BlockSpec((blk_rows, GLA_HEAD_K), lambda b, h, i: (b * nblk + i, kcol + h)),
                  pl.BlockSpec((blk_rows, GLA_HEAD_V), lambda b, h, i: (b * nblk + i, vcol + h)),
                  pl.BlockSpec((blk_rows, GLA_HEAD_V), lambda b, h, i: (b * nblk + i, gcol + h)),
                  pl.BlockSpec((blk_rows, GLA_HEAD_K), lambda b, h, i: (b * nblk + i, h)),
                  pl.BlockSpec((None, 1, GLA_HEAD_V), lambda b, h, i: (layer, 0, 0)),
                  pl.BlockSpec((None, GLA_HEAD_K, GLA_HEAD_V), lambda b, h, i: (h, 0, 0))],
        out_specs=[pl.BlockSpec((blk_rows, GLA_HEAD_V), lambda b, h, i: (b * nblk + i, h)),
                   pl.BlockSpec((None, GLA_HEAD_K, GLA_HEAD_V), lambda b, h, i: (b * GLA_HEADS + h, 0, 0))],
        out_shape=[jax.ShapeDtypeStruct((nbatch * seq, GLA_WIDTH), BF16),
                   jax.ShapeDtypeStruct((nbatch * GLA_HEADS, GLA_HEAD_K, GLA_HEAD_V), F32)],
        scratch_shapes=[pltpu.VMEM((GLA_HEAD_K, GLA_HEAD_V), F32)],
        compiler_params=_params("parallel", "parallel", "arbitrary"),
        name="gla",
    )(p, p, p, p, gk, norm_g, s0)


def _deferred(step, nblocks, produce, finish):
    def emit(produce_parts, finish_parts):
        for i in range(max(len(produce_parts), len(finish_parts))):
            for parts in (produce_parts, finish_parts):
                if i < len(parts):
                    parts[i]()

    @pl.when(step == 0)
    def _():
        emit(produce(0), [])

    for slot in range(2):
        @pl.when(jnp.logical_and(jnp.logical_and(step > 0, step < nblocks), step % 2 == slot))
        def _(slot=slot):
            emit(produce(slot), finish(1 - slot))

    @pl.when(step == nblocks)
    def _():
        emit([], finish((nblocks - 1) % 2))


def _cur_block(s, nblocks):
    return jnp.minimum(s, nblocks - 1)


def _prev_block(s):
    return jnp.maximum(s - 1, 0)


_PROJ_PARTS = 4


def _residual_rows(rows, y_ref, h_ref, gpost_ref, gnext_ref, ho_ref, hn_ref):
    h = h_ref[rows, :] + _rms(y_ref[rows, :], gpost_ref[...])
    ho_ref[rows, :] = h
    hn_ref[rows, :] = _rms(h, gnext_ref[...]).astype(hn_ref.dtype)


def _proj_residual_kernel(*refs, nblocks, k_splits):
    nx = len(k_splits)
    x_refs, (w_ref, h_ref, gpost_ref, gnext_ref, ho_ref, hn_ref, y_scr) = refs[:nx], refs[nx:]
    bm, n = h_ref.shape
    nparts = _PROJ_PARTS if bm % (8 * _PROJ_PARTS) == 0 else 1

    def produce(slot):
        def part(c):
            cols = slice(c * (n // nparts), (c + 1) * (n // nparts))
            y, lo = None, 0
            for x_ref, k in zip(x_refs, k_splits):
                t = jnp.dot(x_ref[...], w_ref[lo:lo + k, cols], preferred_element_type=F32)
                y = t if y is None else y + t
                lo += k
            y_scr[slot, :, cols] = y
        return [functools.partial(part, c) for c in range(nparts)]

    def finish(slot):
        def part(r):
            rows = slice(r * (bm // nparts), (r + 1) * (bm // nparts))
            _residual_rows(rows, y_scr.at[slot], h_ref, gpost_ref, gnext_ref, ho_ref, hn_ref)
        return [functools.partial(part, r) for r in range(nparts)]

    _deferred(pl.program_id(0), nblocks, produce, finish)


def _proj_residual(xs, w, h, g_post, g_next, layer, bm, name):
    m = h.shape[0]
    nblocks = m // bm
    k_splits = tuple(x.shape[1] for x in xs)
    k_total = sum(k_splits)
    cur = lambda s: (_cur_block(s, nblocks), 0)
    prev = lambda s: (_prev_block(s), 0)
    const = lambda s: (0, 0)
    kernel = functools.partial(_proj_residual_kernel, nblocks=nblocks, k_splits=k_splits)
    return pl.pallas_call(
        kernel,
        grid=(nblocks + 1,),
        in_specs=[pl.BlockSpec((bm, k), cur) for k in k_splits] + [
            pl.BlockSpec((None, k_total, D_MODEL), lambda s: (layer, 0, 0), pipeline_mode=pl.Buffered(1)),
            pl.BlockSpec((bm, D_MODEL), prev),
            pl.BlockSpec((1, D_MODEL), const),
            pl.BlockSpec((1, D_MODEL), const)],
        out_specs=[pl.BlockSpec((bm, D_MODEL), prev), pl.BlockSpec((bm, D_MODEL), prev)],
        out_shape=[jax.ShapeDtypeStruct((m, D_MODEL), F32), jax.ShapeDtypeStruct((m, D_MODEL), BF16)],
        scratch_shapes=[pltpu.VMEM((2, bm, D_MODEL), F32)],
        compiler_params=_params("arbitrary"),
        name=name,
    )(*xs, w, h, g_post, g_next)


_HALO = 8
_UP_SUB_ROWS = 256


def _up_conv_kernel(x_ref, xm_ref, wa_ref, wv_ref, cwa_ref, cwv_ref, cba_ref, cbv_ref, o_ref, om_ref,
                    wbf_scr, ubuf, mbuf, *, bm, sub_rows, blocks_per_seq):
    i = pl.program_id(1)
    halves = ((wa_ref, cwa_ref, cba_ref), (wv_ref, cwv_ref, cbv_ref))

    def gated(x, bufs, lo, rows):
        conv = []
        for idx, (_, cw_ref, cb_ref) in enumerate(halves):
            u = jnp.dot(x, wbf_scr[idx], preferred_element_type=F32)
            buf = bufs.at[idx]
            buf[lo:lo + rows, :] = u
            cw = cw_ref[...]
            conv.append(cb_ref[...] + cw[0:1] * buf[lo - 2:lo - 2 + rows, :]
                        + cw[1:2] * buf[lo - 1:lo - 1 + rows, :] + cw[2:3] * u)
        return _silu(conv[0]) * conv[1]

    @pl.when(i == 0)
    def _():
        for idx, (w_ref, _, _) in enumerate(halves):
            wbf_scr[idx] = w_ref[...].astype(BF16)
            mbuf[idx, _HALO - 2:_HALO, :] = jnp.zeros((2, mbuf.shape[2]), F32)
        om_ref[...] = gated(xm_ref[...], mbuf, _HALO, N_META).astype(om_ref.dtype)

    first = (i % blocks_per_seq) == 0
    for idx in range(2):
        buf = ubuf.at[idx]

        @pl.when(first)
        def _():
            buf[_HALO - 2:_HALO, :] = mbuf[idx, _HALO + N_META - 2:_HALO + N_META, :]

        @pl.when(jnp.logical_not(first))
        def _():
            buf[_HALO - 2:_HALO, :] = buf[_HALO + bm - 2:_HALO + bm, :]

    sb = min(bm, sub_rows)
    for r in range(bm // sb):
        rows = slice(r * sb, (r + 1) * sb)
        o_ref[rows, :] = gated(x_ref[rows, :], ubuf, _HALO + r * sb, sb).astype(o_ref.dtype)


def _up_conv(hn, hn_meta, w_up, conv_w, conv_b, layer, bm, bn, blocks_per_seq):
    m = hn.shape[0]
    nj = D_FF // bn
    kernel = functools.partial(_up_conv_kernel, bm=bm, sub_rows=_UP_SUB_ROWS, blocks_per_seq=blocks_per_seq)
    return pl.pallas_call(
        kernel,
        grid=(nj, m // bm),
        in_specs=[pl.BlockSpec((bm, D_MODEL), lambda j, i: (i, 0)),
                  pl.BlockSpec((N_META, D_MODEL), lambda j, i: (0, 0)),
                  pl.BlockSpec((None, D_MODEL, bn), lambda j, i: (layer, 0, j)),
                  pl.BlockSpec((None, D_MODEL, bn), lambda j, i: (layer, 0, j + nj)),
                  pl.BlockSpec((None, 3, bn), lambda j, i: (layer, 0, j)),
                  pl.BlockSpec((None, 3, bn), lambda j, i: (layer, 0, j + nj)),
                  pl.BlockSpec((None, 1, bn), lambda j, i: (layer, 0, j)),
                  pl.BlockSpec((None, 1, bn), lambda j, i: (layer, 0, j + nj))],
        out_specs=[pl.BlockSpec((bm, bn), lambda j, i: (i, j)),
                   pl.BlockSpec((N_META, bn), lambda j, i: (0, j))],
        out_shape=[jax.ShapeDtypeStruct((m, D_FF), BF16),
                   jax.ShapeDtypeStruct((N_META, D_FF), BF16)],
        scratch_shapes=[pltpu.VMEM((2, D_MODEL, bn), BF16),
                        pltpu.VMEM((2, _HALO + bm, bn), F32),
                        pltpu.VMEM((2, _HALO + N_META, bn), F32)],
        compiler_params=_params("parallel", "arbitrary"),
        name="up_conv",
    )(hn, hn_meta, w_up, w_up, conv_w, conv_w, conv_b, conv_b)


def _lam_init(layer):
    return 0.8 - 0.6 * math.exp(-0.3 * layer)


_CFG = dict(bm_norm=512, bm_in=1024, bn_in=1024, bm_gate=1024, tq=512, da_heads=4, gla_chunks=32,
            bm_out=256, bm_up=1024, bn_up=512, bm_down=256)
_META_DA_HEADS = 2


def _trunk(h, hm, prm):
    cfg = _CFG
    hn = _entry_norm(h, prm["pre_mix_g"][0], cfg["bm_norm"])
    hnm = _entry_norm(hm, prm["pre_mix_g"][0], N_META)
    for l in range(DEPTH):
        lam_init = _lam_init(l)
        post_mix_g, pre_ffn_g, post_ffn_g = prm["post_mix_g"][l], prm["pre_ffn_g"][l], prm["post_ffn_g"][l]
        g_next = prm["pre_mix_g"][(l + 1) % DEPTH]
        p, pm = _in_proj(hn, hnm, prm["w_in"], l, cfg["bm_in"], cfg["bn_in"])
        gk = _gate(hn, prm["w_lr"], prm["w2"], prm["b2"], l, cfg["bm_gate"])
        gkm = _gate(hnm, prm["w_lr"], prm["w2"], prm["b2"], l, N_META)
        om_da = _diff_attention(pm, pm, prm["da_lambda"], prm["da_subln_g"], l, lam_init,
                                1, N_META, N_META, _META_DA_HEADS, has_prefix=False)
        s_zero = jnp.zeros((GLA_HEADS, GLA_HEAD_K, GLA_HEAD_V), F32)
        om_gla, s_meta = _gla(pm, gkm, prm["gla_norm_g"], s_zero, l, 1, N_META, N_META, 1)
        o_da = _diff_attention(p, pm, prm["da_lambda"], prm["da_subln_g"], l, lam_init,
                               BATCH, SEQ, cfg["tq"], cfg["da_heads"], has_prefix=True)
        o_gla, _ = _gla(p, gk, prm["gla_norm_g"], s_meta, l, BATCH, SEQ, GLA_CHUNK, cfg["gla_chunks"])
        h, hn = _proj_residual((o_da, o_gla), prm["w_out"], h, post_mix_g, pre_ffn_g, l, cfg["bm_out"], "out_proj")
        hm, hnm = _proj_residual((om_da, om_gla), prm["w_out"], hm, post_mix_g, pre_ffn_g, l, N_META, "out_proj")
        act, actm = _up_conv(hn, hnm, prm["w_up"], prm["conv_w"], prm["conv_b"], l,
                             cfg["bm_up"], cfg["bn_up"], SEQ // cfg["bm_up"])
        h, hn = _proj_residual((act,), prm["w_down"], h, post_ffn_g, g_next, l, cfg["bm_down"], "down_proj")
        if l + 1 < DEPTH:
            hm, hnm = _proj_residual((actm,), prm["w_down"], hm, post_ffn_g, g_next, l, N_META, "down_proj")
    return h


def kernel(x, meta_tokens, pre_mix_g, w_in, da_lambda, da_subln_g, gla_gate_w2, gla_gate_b, gla_norm_g, w_out,
           post_mix_g, pre_ffn_g, w_up, conv_w, conv_b, w_down, post_ffn_g):
    vec = lambda a: a.astype(F32).reshape(DEPTH, 1, a.shape[-1])
    prm = dict(
        pre_mix_g=vec(pre_mix_g), post_mix_g=vec(post_mix_g), pre_ffn_g=vec(pre_ffn_g), post_ffn_g=vec(post_ffn_g),
        w_in=w_in.astype(BF16),
        w_up=w_up.astype(F32),
        w_lr=jnp.pad(w_in[:, :, IN_MAIN:], ((0, 0), (0, 0), (0, LANES - GLA_GATE_RANK))).astype(BF16),
        w2=jnp.pad(gla_gate_w2, ((0, 0), (0, LANES - GLA_GATE_RANK), (0, 0))).astype(BF16),
        b2=vec(gla_gate_b),
        da_lambda=da_lambda.astype(F32), da_subln_g=vec(da_subln_g), gla_norm_g=vec(gla_norm_g),
        w_out=w_out.astype(BF16), w_down=w_down.astype(BF16),
        conv_w=conv_w.astype(F32), conv_b=vec(conv_b),
    )
    h = _trunk(x.astype(F32).reshape(BATCH * SEQ, D_MODEL), meta_tokens.astype(F32), prm)
    return h.reshape(BATCH, SEQ, D_MODEL).astype(x.dtype)
```
